```python
import jax, jax.numpy as jnp
from jax import lax
import numpy as np

D_MODEL = 1024
BATCH = 2
SEQ = 8192
DEPTH = 2

N_MIXERS = 2
N_CONV_LAYERS = (DEPTH + 1) // 2
N_HGRN_LAYERS = DEPTH // 2
CONV_WIDTH = 31
HGRN_EXPAND = 128
HGRN_HEADS = D_MODEL // HGRN_EXPAND
HGRN_KDIM = HGRN_EXPAND
HGRN_VDIM = D_MODEL // HGRN_HEADS
HGRN_FDIM = HGRN_HEADS * HGRN_KDIM
HGRN_IN_WIDTH = 2 * HGRN_FDIM + 2 * D_MODEL
CHUNK = 64
D_FF = 2816
EPS = 1e-6

kernel_name = "hybrid_conformer_conv_hgrn2_macaron"


def rms_norm(x, gain):
    x32 = x.astype(jnp.float32)
    y = x32 * lax.rsqrt(jnp.mean(x32 * x32, axis=-1, keepdims=True) + EPS)
    return (y * gain.astype(jnp.float32)).astype(x.dtype)


def layer_norm(x, gain, bias):
    x32 = x.astype(jnp.float32)
    mu = jnp.mean(x32, axis=-1, keepdims=True)
    xc = x32 - mu
    y = xc * lax.rsqrt(jnp.mean(xc * xc, axis=-1, keepdims=True) + EPS)
    return (y * gain.astype(jnp.float32) + bias.astype(jnp.float32)).astype(x.dtype)


def swiglu_ffn(x, w_in, w_out):
    gate, up = jnp.split(x @ w_in, 2, axis=-1)
    return (jax.nn.silu(gate) * up) @ w_out


def conformer_conv_module(x, w_in, b_in, w_dw, b_dw, ln_g, ln_b, w_out, b_out):
    a, gate = jnp.split(x @ w_in + b_in, 2, axis=-1)
    h = a * jax.nn.sigmoid(gate)
    h = lax.conv_general_dilated(
        h, w_dw[:, None, :].astype(h.dtype),
        window_strides=(1,), padding=[(CONV_WIDTH - 1, 0)],
        dimension_numbers=("NWC", "WIO", "NWC"),
        feature_group_count=D_MODEL) + b_dw
    h = jax.nn.silu(layer_norm(h, ln_g, ln_b))
    return h @ w_out + b_out


def hgrn2_mixer(x, w_in, lower_bound, g_norm_w, w_out):
    bsz, seq, _ = x.shape
    n_chunks = seq // CHUNK
    q, f, i, g = jnp.split(x @ w_in, [HGRN_FDIM, 2 * HGRN_FDIM, 2 * HGRN_FDIM + D_MODEL], axis=-1)
    q = jax.nn.silu(q.astype(jnp.float32))
    f = f.astype(jnp.float32)
    lb = lower_bound.astype(jnp.float32)
    log_f = jnp.logaddexp(jnp.log(lb), jnp.log1p(-lb) + jax.nn.log_sigmoid(f))
    k = (1.0 - lb) * jax.nn.sigmoid(-f)
    v = i.astype(jnp.float32)

    def to_chunks(t, d):
        return t.reshape(bsz, n_chunks, CHUNK, HGRN_HEADS, d).transpose(1, 0, 3, 2, 4)

    qc, kc, gc = to_chunks(q, HGRN_KDIM), to_chunks(k, HGRN_KDIM), to_chunks(log_f, HGRN_KDIM)
    vc = to_chunks(v, HGRN_VDIM)
    causal = jnp.tril(jnp.ones((CHUNK, CHUNK), dtype=bool))[None, None, :, :, None]

    def step(state, inp):
        qb, kb, vb, gb = inp
        b = jnp.cumsum(gb, axis=2)
        o_inter = jnp.einsum("bhtk,bhkv->bhtv", qb * jnp.exp(b), state)
        diff = b[:, :, :, None, :] - b[:, :, None, :, :]
        decay = jnp.exp(jnp.where(causal, diff, -jnp.inf))
        scores = jnp.einsum("bhtk,bhsk,bhtsk->bhts", qb, kb, decay)
        o_intra = jnp.einsum("bhts,bhsv->bhtv", scores, vb)
        b_last = b[:, :, -1]
        k_dec = kb * jnp.exp(b_last[:, :, None, :] - b)
        new_state = jnp.exp(b_last)[..., None] * state + jnp.einsum("bhsk,bhsv->bhkv", k_dec, vb)
        return new_state, o_inter + o_intra

    state0 = jnp.zeros((bsz, HGRN_HEADS, HGRN_KDIM, HGRN_VDIM), jnp.float32)
    _, o = lax.scan(step, state0, (qc, kc, vc, gc))
    o = o.transpose(1, 0, 3, 2, 4).reshape(bsz, seq, HGRN_HEADS, HGRN_VDIM)
    o = rms_norm(o, g_norm_w) * jax.nn.silu(g.astype(jnp.float32)).reshape(bsz, seq, HGRN_HEADS, HGRN_VDIM)
    return o.reshape(bsz, seq, D_MODEL).astype(x.dtype) @ w_out


def setup_inputs(seed: int = 0) -> dict:
    key = jax.random.key(seed)
    ks = jax.random.split(key, 20)
    nrm = lambda k, shape, s: jax.random.normal(k, shape, jnp.float32) * s
    D = D_MODEL
    return {
        "x": nrm(ks[0], (BATCH, SEQ, D), 1.0),
        "norm_gains": 1.0 + nrm(ks[1], (DEPTH, 6, D), 0.02),
        "ffn_w_in": nrm(ks[2], (DEPTH, 2, D, 2 * D_FF), D ** -0.5),
        "ffn_w_out": nrm(ks[3], (DEPTH, 2, D_FF, D), D_FF ** -0.5),
        "conv_w_in": nrm(ks[4], (N_CONV_LAYERS, D, 2 * D), D ** -0.5),
        "conv_b_in": nrm(ks[5], (N_CONV_LAYERS, 2 * D), 0.01),
        "conv_w_dw": nrm(ks[6], (N_CONV_LAYERS, CONV_WIDTH, D), CONV_WIDTH ** -0.5),
        "conv_b_dw": nrm(ks[7], (N_CONV_LAYERS, D), 0.01),
        "conv_ln_g": 1.0 + nrm(ks[8], (N_CONV_LAYERS, D), 0.02),
        "conv_ln_b": nrm(ks[9], (N_CONV_LAYERS, D), 0.01),
        "conv_w_out": nrm(ks[10], (N_CONV_LAYERS, D, D), D ** -0.5),
        "conv_b_out": nrm(ks[11], (N_CONV_LAYERS, D), 0.01),
        "hgrn_w_in": nrm(ks[12], (N_HGRN_LAYERS, D, HGRN_IN_WIDTH), D ** -0.5),
        "hgrn_lb_logits": nrm(ks[13], (DEPTH, HGRN_FDIM), 0.1),
        "hgrn_g_norm": 1.0 + nrm(ks[14], (N_HGRN_LAYERS, HGRN_VDIM), 0.02),
        "hgrn_w_out": nrm(ks[15], (N_HGRN_LAYERS, D, D), D ** -0.5),
    }


def reference(x, norm_gains, ffn_w_in, ffn_w_out, conv_w_in, conv_b_in, conv_w_dw, conv_b_dw,
              conv_ln_g, conv_ln_b, conv_w_out, conv_b_out, hgrn_w_in, hgrn_lb_logits,
              hgrn_g_norm, hgrn_w_out):
    p = jax.nn.softmax(hgrn_lb_logits.astype(jnp.float32), axis=0)
    lower_bounds = jnp.cumsum(p, axis=0) - p[0:1]

    for layer in range(DEPTH):
        g = norm_gains[layer]
        h = swiglu_ffn(rms_norm(x, g[0]), ffn_w_in[layer, 0], ffn_w_out[layer, 0])
        x = x + 0.5 * rms_norm(h, g[1])
        hn = rms_norm(x, g[2])
        j = layer // N_MIXERS
        if layer % N_MIXERS == 0:
            m = conformer_conv_module(hn, conv_w_in[j], conv_b_in[j], conv_w_dw[j], conv_b_dw[j],
                                      conv_ln_g[j], conv_ln_b[j], conv_w_out[j], conv_b_out[j])
        else:
            m = hgrn2_mixer(hn, hgrn_w_in[j], lower_bounds[layer], hgrn_g_norm[j], hgrn_w_out[j])
        x = x + rms_norm(m, g[3])
        h = swiglu_ffn(rms_norm(x, g[4]), ffn_w_in[layer, 1], ffn_w_out[layer, 1])
        x = x + 0.5 * rms_norm(h, g[5])
    return x
```

```python
import functools

import jax
import jax.numpy as jnp
import numpy as np
from jax import lax
from jax.experimental import pallas as pl
from jax.experimental.pallas import tpu as pltpu

EPS = 1e-6
CONV_WIDTH = 31
HGRN_HEADS = 8
HEAD_DIM = 128

V7X_SUBLANES = 8
V7X_VMEM_BYTES = 64 * 1024 * 1024
VMEM_LIMIT_BYTES = 56 * 1024 * 1024

FFN_ROWS = 512
FFN_COLS = 256
CONV_ROWS = 256
CONV_HALO = 32
CONV_SUB = 32
HGRN_ROWS = 256
HGRN_CHUNK = 64
NEG_BIG = -1e30


def _rms(x, gain):
    return x * lax.rsqrt(jnp.mean(x * x, axis=-1, keepdims=True) + EPS) * gain


def _sigmoid(x):
    return 1.0 / (1.0 + jnp.exp(-x))


def _bdot(a, b):
    return jnp.dot(a, b, preferred_element_type=jnp.float32)


def _resident(shape):
    return pl.BlockSpec(shape, lambda *_: (0,) * len(shape), pipeline_mode=pl.Buffered(1))


def _ffn_kernel(x_ref, g_in_ref, g_out_ref, w_in_ref, w_out_ref, o_ref, act_ref, *, d_ff):
    x = x_ref[...]
    xn = _rms(x, g_in_ref[...]).astype(jnp.bfloat16)
    for c0 in range(0, d_ff, FFN_COLS):
        gate = _bdot(xn, w_in_ref[:, c0:c0 + FFN_COLS])
        up = _bdot(xn, w_in_ref[:, d_ff + c0:d_ff + c0 + FFN_COLS])
        act_ref[:, c0:c0 + FFN_COLS] = (gate * _sigmoid(gate) * up).astype(jnp.bfloat16)
    h = _bdot(act_ref[...], w_out_ref[...])
    o_ref[...] = x + 0.5 * _rms(h, g_out_ref[...])


def _ffn(x2d, g_in, g_out, w_in, w_out):
    m, d = x2d.shape
    d_ff = w_out.shape[0]
    assert m % FFN_ROWS == 0 and d_ff % FFN_COLS == 0
    row = pl.BlockSpec((FFN_ROWS, d), lambda i: (i, 0))
    return pl.pallas_call(
        functools.partial(_ffn_kernel, d_ff=d_ff),
        grid=(m // FFN_ROWS,),
        in_specs=[row, _resident((1, d)), _resident((1, d)),
                  _resident(w_in.shape), _resident(w_out.shape)],
        out_specs=row,
        out_shape=jax.ShapeDtypeStruct((m, d), jnp.float32),
        scratch_shapes=[pltpu.VMEM((FFN_ROWS, d_ff), jnp.bfloat16)],
        compiler_params=pltpu.CompilerParams(
            dimension_semantics=("arbitrary",), vmem_limit_bytes=VMEM_LIMIT_BYTES),
        name="ffn",
    )(x2d, g_in, g_out, w_in, w_out)


def _conv_kernel(x_ref, g_pre_ref, g_post_ref, w_in_ref, b_in_ref, w_dw_ref, b_dw_ref,
                 ln_g_ref, ln_b_ref, w_out_ref, b_out_ref, o_ref, hext_ref, conv_ref):
    d = x_ref.shape[-1]
    rows = x_ref.shape[1]
    first = pl.program_id(1) == 0

    @pl.when(first)
    def _():
        hext_ref[0:CONV_HALO, :] = jnp.zeros((CONV_HALO, d), jnp.float32)

    @pl.when(jnp.logical_not(first))
    def _():
        hext_ref[0:CONV_HALO, :] = hext_ref[rows:rows + CONV_HALO, :]

    x = x_ref[0]
    xn = _rms(x, g_pre_ref[...]).astype(jnp.bfloat16)
    y = _bdot(xn, w_in_ref[...]) + b_in_ref[...]
    hext_ref[CONV_HALO:CONV_HALO + rows, :] = y[:, :d] * _sigmoid(y[:, d:])

    shift = CONV_HALO - (CONV_WIDTH - 1)
    for r0 in range(0, rows, CONV_SUB):
        acc = jnp.zeros((CONV_SUB, d), jnp.float32)
        for j in range(CONV_WIDTH):
            lo = r0 + shift + j
            acc = acc + w_dw_ref[j:j + 1, :] * hext_ref[lo:lo + CONV_SUB, :]
        conv_ref[r0:r0 + CONV_SUB, :] = acc

    h = conv_ref[...] + b_dw_ref[...]
    mu = jnp.mean(h, axis=-1, keepdims=True)
    hc = h - mu
    hn = hc * lax.rsqrt(jnp.mean(hc * hc, axis=-1, keepdims=True) + EPS)
    hn = hn * ln_g_ref[...] + ln_b_ref[...]
    act = (hn * _sigmoid(hn)).astype(jnp.bfloat16)
    m = _bdot(act, w_out_ref[...]) + b_out_ref[...]
    o_ref[0] = x + _rms(m, g_post_ref[...])


def _conv_mixer(x, g_pre, g_post, w_in, b_in, w_dw, b_dw, ln_g, ln_b, w_out, b_out):
    bsz, seq, d = x.shape
    assert seq % CONV_ROWS == 0 and CONV_ROWS % CONV_SUB == 0
    assert CONV_HALO >= CONV_WIDTH - 1 and CONV_HALO % V7X_SUBLANES == 0
    row = pl.BlockSpec((1, CONV_ROWS, d), lambda b, s: (b, s, 0))
    vec = _resident((1, d))
    return pl.pallas_call(
        _conv_kernel,
        grid=(bsz, seq // CONV_ROWS),
        in_specs=[row, vec, vec, _resident(w_in.shape), _resident((1, 2 * d)),
                  _resident(w_dw.shape), vec, vec, vec, _resident(w_out.shape), vec],
        out_specs=row,
        out_shape=jax.ShapeDtypeStruct(x.shape, jnp.float32),
        scratch_shapes=[pltpu.VMEM((CONV_HALO + CONV_ROWS, d), jnp.float32),
                        pltpu.VMEM((CONV_ROWS, d), jnp.float32)],
        compiler_params=pltpu.CompilerParams(
            dimension_semantics=("arbitrary", "arbitrary"), vmem_limit_bytes=VMEM_LIMIT_BYTES),
        name="conv_mixer",
    )(x, g_pre, g_post, w_in, b_in, w_dw, b_dw, ln_g, ln_b, w_out, b_out)


HGRN_LEVELS = tuple(HGRN_CHUNK >> (i + 1) for i in range(HGRN_CHUNK.bit_length() - 1))


def _hgrn_sum_matrix():
    c = HGRN_CHUNK
    t = np.arange(c)[:, None]
    u = np.arange(c)[None, :]
    blocks = [u <= t, u > t]
    for h in HGRN_LEVELS:
        mid = (t // (2 * h)) * (2 * h) + h
        upper = t >= mid
        blocks.append(np.where(upper, (u >= mid) & (u <= t), (u > t) & (u < mid)))
    return np.concatenate(blocks, axis=0).astype(np.float32)


def _split3(x):
    hi = x.astype(jnp.bfloat16)
    r1 = x - hi.astype(jnp.float32)
    mid = r1.astype(jnp.bfloat16)
    lo = (r1 - mid.astype(jnp.float32)).astype(jnp.bfloat16)
    return hi, mid, lo


def _hgrn_kernel(x_ref, g_pre_ref, g_post_ref, w_in_ref, lb_logits_ref, g_norm_ref, w_out_ref,
                 sum_ref, o_ref, q_ref, k_ref, lf_ref, v_ref, out_ref, state_ref, *, layer):
    d = x_ref.shape[-1]
    rows = x_ref.shape[1]
    c = HGRN_CHUNK

    @pl.when(pl.program_id(1) == 0)
    def _():
        state_ref[...] = jnp.zeros(state_ref.shape, jnp.float32)

    logits = lb_logits_ref[...]
    e = jnp.exp(logits - jnp.max(logits, axis=0, keepdims=True))
    p = e / jnp.sum(e, axis=0, keepdims=True)
    cum = p[0:1]
    for l in range(1, layer + 1):
        cum = cum + p[l:l + 1]
    lb = cum - p[0:1]

    x = x_ref[0]
    xn = _rms(x, g_pre_ref[...]).astype(jnp.bfloat16)
    y = _bdot(xn, w_in_ref[...])
    q = y[:, 0:d]
    f = y[:, d:2 * d]
    q_ref[...] = q * _sigmoid(q)
    log_sig = -(jnp.maximum(-f, 0.0) + jnp.log1p(jnp.exp(-jnp.abs(f))))
    a1 = jnp.log(lb)
    a2 = jnp.log1p(-lb) + log_sig
    lf_ref[...] = jnp.maximum(a1, a2) + jnp.log1p(jnp.exp(-jnp.abs(a1 - a2)))
    k_ref[...] = (1.0 - lb) * _sigmoid(-f)
    v_ref[...] = y[:, 2 * d:3 * d].astype(jnp.bfloat16)
    gate = y[:, 3 * d:4 * d]
    gate = gate * _sigmoid(gate)

    row_id = lax.broadcasted_iota(jnp.int32, (c, d), 0)
    tt = lax.broadcasted_iota(jnp.int32, (c, c), 0)
    ss = lax.broadcasted_iota(jnp.int32, (c, c), 1)

    def chunk_body(ci, carry):
        r0 = pl.multiple_of(ci * c, c)
        qc = q_ref[pl.ds(r0, c), :]
        kc = k_ref[pl.ds(r0, c), :]
        vc = v_ref[pl.ds(r0, c), :]
        hi, mid, lo = _split3(lf_ref[pl.ds(r0, c), :])
        ex = _bdot(sum_ref[...], jnp.concatenate([hi, mid, lo], axis=0))
        q_dec = (qc * jnp.exp(ex[0:c])).astype(jnp.bfloat16)
        k_dec = (kc * jnp.exp(ex[c:2 * c])).astype(jnp.bfloat16)
        decay_last = jnp.exp(ex[c - 1:c])
        q_lv, k_lv = [qc.astype(jnp.bfloat16)], [kc.astype(jnp.bfloat16)]
        for i, h in enumerate(HGRN_LEVELS):
            fac = jnp.exp(ex[(2 + i) * c:(3 + i) * c])
            upper = (row_id & h) != 0
            q_lv.append((qc * jnp.where(upper, fac, 0.0)).astype(jnp.bfloat16))
            k_lv.append((kc * jnp.where(upper, 0.0, fac)).astype(jnp.bfloat16))
        blk = lambda h: (2 * h).bit_length() - 1
        masks = [tt == ss] + [(tt >> blk(h)) == (ss >> blk(h)) for h in HGRN_LEVELS]
        for hd in range(HGRN_HEADS):
            sl = slice(hd * HEAD_DIM, (hd + 1) * HEAD_DIM)
            scores = jnp.zeros((c, c), jnp.float32)
            for ql, kl, mask in zip(q_lv, k_lv, masks):
                s_l = lax.dot_general(ql[:, sl], kl[:, sl], (((1,), (1,)), ((), ())),
                                      preferred_element_type=jnp.float32)
                scores = scores + jnp.where(mask, s_l, 0.0)
            state_t = state_ref[hd]
            o = lax.dot_general(q_dec[:, sl], state_t.astype(jnp.bfloat16),
                                (((1,), (1,)), ((), ())), preferred_element_type=jnp.float32)
            o = o + _bdot(scores.astype(jnp.bfloat16), vc[:, sl])
            out_ref[pl.ds(r0, c), sl] = o
            upd = lax.dot_general(vc[:, sl], k_dec[:, sl], (((0,), (0,)), ((), ())),
                                  preferred_element_type=jnp.float32)
            state_ref[hd] = state_t * decay_last[:, sl] + upd
        return carry

    lax.fori_loop(0, rows // c, chunk_body, 0)

    o = out_ref[...]
    parts = []
    for hd in range(HGRN_HEADS):
        oh = o[:, hd * HEAD_DIM:(hd + 1) * HEAD_DIM]
        parts.append(oh * lax.rsqrt(jnp.mean(oh * oh, axis=-1, keepdims=True) + EPS))
    on = jnp.concatenate(parts, axis=-1) * g_norm_ref[...] * gate
    m = _bdot(on.astype(jnp.bfloat16), w_out_ref[...])
    o_ref[0] = x + _rms(m, g_post_ref[...])


def _hgrn_mixer(x, g_pre, g_post, w_in, lb_logits, g_norm, w_out, *, layer):
    bsz, seq, d = x.shape
    assert seq % HGRN_ROWS == 0 and HGRN_ROWS % HGRN_CHUNK == 0
    assert d == HGRN_HEADS * HEAD_DIM and w_in.shape == (d, 4 * d)
    sum_mat = _hgrn_sum_matrix()
    sum3 = jnp.asarray(np.concatenate([sum_mat] * 3, axis=1), jnp.bfloat16)
    g_norm_row = jnp.tile(g_norm.reshape(1, HEAD_DIM), (1, HGRN_HEADS))
    row = pl.BlockSpec((1, HGRN_ROWS, d), lambda b, s: (b, s, 0))
    vec = _resident((1, d))
    return pl.pallas_call(
        functools.partial(_hgrn_kernel, layer=layer),
        grid=(bsz, seq // HGRN_ROWS),
        in_specs=[row, vec, vec, _resident(w_in.shape), _resident(lb_logits.shape), vec,
                  _resident(w_out.shape), _resident(sum3.shape)],
        out_specs=row,
        out_shape=jax.ShapeDtypeStruct(x.shape, jnp.float32),
        scratch_shapes=[pltpu.VMEM((HGRN_ROWS, d), jnp.float32),
                        pltpu.VMEM((HGRN_ROWS, d), jnp.float32),
                        pltpu.VMEM((HGRN_ROWS, d), jnp.float32),
                        pltpu.VMEM((HGRN_ROWS, d), jnp.bfloat16),
                        pltpu.VMEM((HGRN_ROWS, d), jnp.float32),
                        pltpu.VMEM((HGRN_HEADS, HEAD_DIM, HEAD_DIM), jnp.float32)],
        compiler_params=pltpu.CompilerParams(
            dimension_semantics=("arbitrary", "arbitrary"), vmem_limit_bytes=VMEM_LIMIT_BYTES),
        name="hgrn_mixer",
    )(x, g_pre, g_post, w_in, lb_logits, g_norm_row, w_out, sum3)


def kernel(x, norm_gains, ffn_w_in, ffn_w_out, conv_w_in, conv_b_in, conv_w_dw, conv_b_dw,
           conv_ln_g, conv_ln_b, conv_w_out, conv_b_out, hgrn_w_in, hgrn_lb_logits,
           hgrn_g_norm, hgrn_w_out):
    bsz, seq, d = x.shape
    depth = norm_gains.shape[0]
    bf = lambda w: w.astype(jnp.bfloat16)
    vec = lambda v: v.reshape(1, -1).astype(jnp.float32)

    def ffn(x, layer, which):
        g = norm_gains[layer]
        y = _ffn(x.reshape(bsz * seq, d), vec(g[4 * which]), vec(g[4 * which + 1]),
                 bf(ffn_w_in[layer, which]), bf(ffn_w_out[layer, which]))
        return y.reshape(bsz, seq, d)

    for layer in range(depth):
        g = norm_gains[layer]
        x = ffn(x, layer, 0)
        j = layer // 2
        if layer % 2 == 0:
            x = _conv_mixer(x, vec(g[2]), vec(g[3]), bf(conv_w_in[j]), vec(conv_b_in[j]),
                            conv_w_dw[j], vec(conv_b_dw[j]), vec(conv_ln_g[j]),
                            vec(conv_ln_b[j]), bf(conv_w_out[j]), vec(conv_b_out[j]))
        else:
            x = _hgrn_mixer(x, vec(g[2]), vec(g[3]), bf(hgrn_w_in[j]), hgrn_lb_logits,
                            hgrn_g_norm[j], bf(hgrn_w_out[j]), layer=layer)
        x = ffn(x, layer, 1)
    return x
```

```python
import functools

import jax
import jax.numpy as jnp
import numpy as np
from jax import lax
from jax.experimental import pallas as pl
from jax.experimental.pallas import tpu as pltpu

EPS = 1e-6
CONV_WIDTH = 31
HGRN_HEADS = 8
HEAD_DIM = 128

V7X_SUBLANES = 8
V7X_VMEM_BYTES = 64 * 1024 * 1024
VMEM_LIMIT_BYTES = 56 * 1024 * 1024

FFN_ROWS = 1024
FFN_COLS = 256
CONV_ROWS = 256
CONV_HALO = 32
CONV_SUB = 32
CONV_LANES = 512
HGRN_ROWS = 256
HGRN_CHUNK = 64


def _rms(x, gain):
    return x * lax.rsqrt(jnp.mean(x * x, axis=-1, keepdims=True) + EPS) * gain


def _sigmoid(x):
    return 1.0 / (1.0 + jnp.exp(-x))


def _bdot(a, b):
    return jnp.dot(a, b, preferred_element_type=jnp.float32)


def _resident(shape):
    return pl.BlockSpec(shape, lambda *_: (0,) * len(shape), pipeline_mode=pl.Buffered(1))


def _ffn_kernel(x_ref, g_in_ref, g_out_ref, w_in_ref, w_out_ref, o_ref, act_ref, *, d_ff):
    x = x_ref[...]
    xn = _rms(x, g_in_ref[...]).astype(jnp.bfloat16)
    for c0 in range(0, d_ff, FFN_COLS):
        gate = _bdot(xn, w_in_ref[:, c0:c0 + FFN_COLS])
        up = _bdot(xn, w_in_ref[:, d_ff + c0:d_ff + c0 + FFN_COLS])
        act_ref[:, c0:c0 + FFN_COLS] = (gate * _sigmoid(gate) * up).astype(jnp.bfloat16)
    h = _bdot(act_ref[...], w_out_ref[...])
    o_ref[...] = x + 0.5 * _rms(h, g_out_ref[...])


def _ffn(x2d, g_in, g_out, w_in, w_out):
    m, d = x2d.shape
    d_ff = w_out.shape[0]
    assert m % FFN_ROWS == 0 and d_ff % FFN_COLS == 0
    row = pl.BlockSpec((FFN_ROWS, d), lambda i: (i, 0))
    return pl.pallas_call(
        functools.partial(_ffn_kernel, d_ff=d_ff),
        grid=(m // FFN_ROWS,),
        in_specs=[row, _resident((1, d)), _resident((1, d)),
                  _resident(w_in.shape), _resident(w_out.shape)],
        out_specs=row,
        out_shape=jax.ShapeDtypeStruct((m, d), jnp.float32),
        scratch_shapes=[pltpu.VMEM((FFN_ROWS, d_ff), jnp.bfloat16)],
        compiler_params=pltpu.CompilerParams(
            dimension_semantics=("arbitrary",), vmem_limit_bytes=VMEM_LIMIT_BYTES),
        name="ffn",
    )(x2d, g_in, g_out, w_in, w_out)


def _conv_kernel(x_ref, g_pre_ref, g_post_ref, w_in_ref, b_in_ref, w_dw_ref, b_dw_ref,
                 ln_g_ref, ln_b_ref, w_out_ref, b_out_ref, o_ref, hext_ref, shift_ref, taps_ref, conv_ref):
    d = x_ref.shape[-1]
    rows = x_ref.shape[1]
    n_ext = CONV_HALO + rows
    first = pl.program_id(1) == 0

    @pl.when(first)
    def _():
        hext_ref[0:CONV_HALO, :] = jnp.zeros((CONV_HALO, d), jnp.float32)
        hext_ref[n_ext:n_ext + V7X_SUBLANES, :] = jnp.zeros((V7X_SUBLANES, d), jnp.float32)

    @pl.when(jnp.logical_not(first))
    def _():
        hext_ref[0:CONV_HALO, :] = hext_ref[rows:n_ext, :]

    x = x_ref[0]
    xn = _rms(x, g_pre_ref[...]).astype(jnp.bfloat16)
    y = _bdot(xn, w_in_ref[...]) + b_in_ref[...]
    hext_ref[CONV_HALO:n_ext, :] = y[:, :d] * _sigmoid(y[:, d:])

    sub = lax.broadcasted_iota(jnp.int32, (V7X_SUBLANES, d), 0)
    n_tiles = n_ext // V7X_SUBLANES
    for s in range(1, V7X_SUBLANES):
        rot = [pltpu.roll(hext_ref[i * V7X_SUBLANES:(i + 1) * V7X_SUBLANES, :],
                          V7X_SUBLANES - s, axis=0) for i in range(n_tiles + 1)]
        for i in range(n_tiles):
            shift_ref[s - 1, i * V7X_SUBLANES:(i + 1) * V7X_SUBLANES, :] = jnp.where(
                sub < V7X_SUBLANES - s, rot[i], rot[i + 1])

    for j in range(CONV_WIDTH):
        taps_ref[j] = jnp.broadcast_to(w_dw_ref[j:j + 1, :], (V7X_SUBLANES, d))

    for c0 in range(0, d, CONV_LANES):
        for r0 in range(0, rows, CONV_SUB):
            acc = [jnp.zeros((V7X_SUBLANES, CONV_LANES), jnp.float32)
                   for _ in range(CONV_SUB // V7X_SUBLANES)]
            for j in range(CONV_WIDTH):
                off = CONV_HALO - (CONV_WIDTH - 1) + j
                lo = r0 + off - off % V7X_SUBLANES
                tap = taps_ref[j, :, c0:c0 + CONV_LANES]
                for i in range(len(acc)):
                    r = lo + i * V7X_SUBLANES
                    if off % V7X_SUBLANES == 0:
                        win = hext_ref[r:r + V7X_SUBLANES, c0:c0 + CONV_LANES]
                    else:
                        win = shift_ref[off % V7X_SUBLANES - 1, r:r + V7X_SUBLANES,
                                        c0:c0 + CONV_LANES]
                    acc[i] = acc[i] + tap * win
            for i in range(len(acc)):
                r = r0 + i * V7X_SUBLANES
                conv_ref[r:r + V7X_SUBLANES, c0:c0 + CONV_LANES] = acc[i]

    h = conv_ref[...] + b_dw_ref[...]
    mu = jnp.mean(h, axis=-1, keepdims=True)
    hc = h - mu
    hn = hc * lax.rsqrt(jnp.mean(hc * hc, axis=-1, keepdims=True) + EPS)
    hn = hn * ln_g_ref[...] + ln_b_ref[...]
    act = (hn * _sigmoid(hn)).astype(jnp.bfloat16)
    m = _bdot(act, w_out_ref[...]) + b_out_ref[...]
    o_ref[0] = x + _rms(m, g_post_ref[...])


def _conv_mixer(x, g_pre, g_post, w_in, b_in, w_dw, b_dw, ln_g, ln_b, w_out, b_out):
    bsz, seq, d = x.shape
    assert seq % CONV_ROWS == 0 and CONV_ROWS % CONV_SUB == 0
    assert CONV_HALO >= CONV_WIDTH - 1 and CONV_HALO % V7X_SUBLANES == 0
    n_ext = CONV_HALO + CONV_ROWS
    row = pl.BlockSpec((1, CONV_ROWS, d), lambda b, s: (b, s, 0))
    vec = _resident((1, d))
    return pl.pallas_call(
        _conv_kernel,
        grid=(bsz, seq // CONV_ROWS),
        in_specs=[row, vec, vec, _resident(w_in.shape), _resident((1, 2 * d)),
                  _resident(w_dw.shape), vec, vec, vec, _resident(w_out.shape), vec],
        out_specs=row,
        out_shape=jax.ShapeDtypeStruct(x.shape, jnp.float32),
        scratch_shapes=[pltpu.VMEM((n_ext + V7X_SUBLANES, d), jnp.float32),
                        pltpu.VMEM((V7X_SUBLANES - 1, n_ext, d), jnp.float32),
                        pltpu.VMEM((CONV_WIDTH, V7X_SUBLANES, d), jnp.float32),
                        pltpu.VMEM((CONV_ROWS, d), jnp.float32)],
        compiler_params=pltpu.CompilerParams(
            dimension_semantics=("arbitrary", "arbitrary"), vmem_limit_bytes=VMEM_LIMIT_BYTES),
        name="conv_mixer",
    )(x, g_pre, g_post, w_in, b_in, w_dw, b_dw, ln_g, ln_b, w_out, b_out)


HGRN_LEVELS = tuple(HGRN_CHUNK >> (i + 1) for i in range(HGRN_CHUNK.bit_length() - 1))
LOG2_E = 1.4426950408889634


def _hgrn_sum_matrix():
    c = HGRN_CHUNK
    t = np.arange(c)[:, None]
    u = np.arange(c)[None, :]
    blocks = [u <= t]
    for h in HGRN_LEVELS:
        mid = (t // (2 * h)) * (2 * h) + h
        upper = t >= mid
        blocks.append(np.where(upper, (u >= mid) & (u <= t), (u > t) & (u < mid)))
    return np.concatenate(blocks, axis=0).astype(np.float32)


def _split3(x):
    hi = x.astype(jnp.bfloat16)
    r1 = x - hi.astype(jnp.float32)
    mid = r1.astype(jnp.bfloat16)
    lo = (r1 - mid.astype(jnp.float32)).astype(jnp.bfloat16)
    return hi, mid, lo


def _dot_nt(a, b):
    return lax.dot_general(a, b, (((1,), (1,)), ((), ())), preferred_element_type=jnp.float32)


def _dot_tn(a, b):
    return lax.dot_general(a, b, (((0,), (0,)), ((), ())), preferred_element_type=jnp.float32)


def _hgrn_kernel(x_ref, g_pre_ref, g_post_ref, w_in_ref, lb_logits_ref, g_norm_ref, w_out_ref,
                 sum_ref, o_ref, state_ref, *, layer):
    d = x_ref.shape[-1]
    rows = x_ref.shape[1]
    c = HGRN_CHUNK
    n_chunks = rows // c
    heads = [slice(hd * HEAD_DIM, (hd + 1) * HEAD_DIM) for hd in range(HGRN_HEADS)]

    @pl.when(pl.program_id(1) == 0)
    def _():
        state_ref[...] = jnp.zeros(state_ref.shape, jnp.float32)

    logits = lb_logits_ref[...]
    e = jnp.exp(logits - jnp.max(logits, axis=0, keepdims=True))
    p = e / jnp.sum(e, axis=0, keepdims=True)
    cum = p[0:1]
    for l in range(1, layer + 1):
        cum = cum + p[l:l + 1]
    lb = cum - p[0:1]

    x = x_ref[0]
    xn = _rms(x, g_pre_ref[...]).astype(jnp.bfloat16)
    y = _bdot(xn, w_in_ref[...])
    q = y[:, 0:d]
    f = y[:, d:2 * d]
    q = q * _sigmoid(q)
    t = jnp.exp(-jnp.abs(f))
    w = 1.0 + t
    r = 1.0 / w
    a1 = jnp.log(lb)
    a2 = jnp.log1p(-lb) + (jnp.minimum(f, 0.0) - jnp.log(w))
    log_f = jnp.maximum(a1, a2) + jnp.log(1.0 + jnp.exp(-jnp.abs(a1 - a2)))
    k = (1.0 - lb) * jnp.where(f >= 0.0, t * r, r)
    v = y[:, 2 * d:3 * d].astype(jnp.bfloat16)
    lf2 = log_f * LOG2_E

    tt = lax.broadcasted_iota(jnp.int32, (c, c), 0)
    ss = lax.broadcasted_iota(jnp.int32, (c, c), 1)
    masks = [tt == ss]
    for h in HGRN_LEVELS:
        shift = (2 * h).bit_length() - 1
        masks.append(((tt >> shift) == (ss >> shift)) & ((tt & h) != 0) & ((ss & h) == 0))

    def exponents(ci):
        hi, mid, lo = _split3(lf2[ci * c:(ci + 1) * c])
        return _bdot(sum_ref[...], jnp.concatenate([hi, mid, lo], axis=0))

    def chunk_scores(ci, ex):
        qc = q[ci * c:(ci + 1) * c]
        kc = k[ci * c:(ci + 1) * c]
        b = ex[0:c]
        b_last = b[c - 1:c]
        q_dec = (qc * jnp.exp2(b)).astype(jnp.bfloat16)
        k_dec = (kc * jnp.exp2(b_last - b)).astype(jnp.bfloat16)
        q_lv, k_lv = [qc.astype(jnp.bfloat16)], [kc.astype(jnp.bfloat16)]
        for i in range(len(HGRN_LEVELS)):
            fac = jnp.exp2(ex[(1 + i) * c:(2 + i) * c])
            q_lv.append((qc * fac).astype(jnp.bfloat16))
            k_lv.append((kc * fac).astype(jnp.bfloat16))
        scores = []
        for sl in heads:
            acc = jnp.zeros((c, c), jnp.float32)
            for ql, kl, mask in zip(q_lv, k_lv, masks):
                acc = acc + jnp.where(mask, _dot_nt(ql[:, sl], kl[:, sl]), 0.0)
            scores.append(acc.astype(jnp.bfloat16))
        return scores, q_dec, k_dec, jnp.exp2(b_last)

    ex_next = exponents(0)
    per_chunk = []
    for ci in range(n_chunks):
        ex = ex_next
        if ci + 1 < n_chunks:
            ex_next = exponents(ci + 1)
        per_chunk.append(chunk_scores(ci, ex))

    o_inter = []
    for ci, (_, q_dec, k_dec, decay_last) in enumerate(per_chunk):
        vc = v[ci * c:(ci + 1) * c]
        outs = []
        for hd, sl in enumerate(heads):
            state_t = state_ref[hd]
            outs.append(_dot_nt(q_dec[:, sl], state_t.astype(jnp.bfloat16)))
            state_ref[hd] = state_t * decay_last[:, sl] + _dot_tn(vc[:, sl], k_dec[:, sl])
        o_inter.append(outs)

    o_rows = []
    for ci, (scores, _, _, _) in enumerate(per_chunk):
        vc = v[ci * c:(ci + 1) * c]
        o_rows.append(jnp.concatenate(
            [o_inter[ci][hd] + _bdot(scores[hd], vc[:, sl]) for hd, sl in enumerate(heads)],
            axis=-1))
    o = jnp.concatenate(o_rows, axis=0)

    gate = y[:, 3 * d:4 * d]
    gate = gate * _sigmoid(gate)
    parts = []
    for sl in heads:
        oh = o[:, sl]
        parts.append(oh * lax.rsqrt(jnp.mean(oh * oh, axis=-1, keepdims=True) + EPS))
    on = jnp.concatenate(parts, axis=-1) * g_norm_ref[...] * gate
    m = _bdot(on.astype(jnp.bfloat16), w_out_ref[...])
    o_ref[0] = x + _rms(m, g_post_ref[...])


def _hgrn_mixer(x, g_pre, g_post, w_in, lb_logits, g_norm, w_out, *, layer):
    bsz, seq, d = x.shape
    assert seq % HGRN_ROWS == 0 and HGRN_ROWS % HGRN_CHUNK == 0
    assert d == HGRN_HEADS * HEAD_DIM and w_in.shape == (d, 4 * d)
    sum_mat = _hgrn_sum_matrix()
    sum3 = jnp.asarray(np.concatenate([sum_mat] * 3, axis=1), jnp.bfloat16)
    g_norm_row = jnp.tile(g_norm.reshape(1, HEAD_DIM), (1, HGRN_HEADS))
    row = pl.BlockSpec((1, HGRN_ROWS, d), lambda b, s: (b, s, 0))
    vec = _resident((1, d))
    return pl.pallas_call(
        functools.partial(_hgrn_kernel, layer=layer),
        grid=(bsz, seq // HGRN_ROWS),
        in_specs=[row, vec, vec, _resident(w_in.shape), _resident(lb_logits.shape), vec,
                  _resident(w_out.shape), _resident(sum3.shape)],
        out_specs=row,
        out_shape=jax.ShapeDtypeStruct(x.shape, jnp.float32),
        scratch_shapes=[pltpu.VMEM((HGRN_HEADS, HEAD_DIM, HEAD_DIM), jnp.float32)],
        compiler_params=pltpu.CompilerParams(
            dimension_semantics=("arbitrary", "arbitrary"), vmem_limit_bytes=VMEM_LIMIT_BYTES),
        name="hgrn_mixer",
    )(x, g_pre, g_post, w_in, lb_logits, g_norm_row, w_out, sum3)


def kernel(x, norm_gains, ffn_w_in, ffn_w_out, conv_w_in, conv_b_in, conv_w_dw, conv_b_dw,
           conv_ln_g, conv_ln_b, conv_w_out, conv_b_out, hgrn_w_in, hgrn_lb_logits,
           hgrn_g_norm, hgrn_w_out):
    bsz, seq, d = x.shape
    depth = norm_gains.shape[0]
    bf = lambda w: w.astype(jnp.bfloat16)
    vec = lambda v: v.reshape(1, -1).astype(jnp.float32)

    def ffn(x, layer, which):
        g = norm_gains[layer]
        y = _ffn(x.reshape(bsz * seq, d), vec(g[4 * which]), vec(g[4 * which + 1]),
                 bf(ffn_w_in[layer, which]), bf(ffn_w_out[layer, which]))
        return y.reshape(bsz, seq, d)

    for layer in range(depth):
        g = norm_gains[layer]
        x = ffn(x, layer, 0)
        j = layer // 2
        if layer % 2 == 0:
            x = _conv_mixer(x, vec(g[2]), vec(g[3]), bf(conv_w_in[j]), vec(conv_b_in[j]),
                            conv_w_dw[j], vec(conv_b_dw[j]), vec(conv_ln_g[j]),
                            vec(conv_ln_b[j]), bf(conv_w_out[j]), vec(conv_b_out[j]))
        else:
            x = _hgrn_mixer(x, vec(g[2]), vec(g[3]), bf(hgrn_w_in[j]), hgrn_lb_logits,
                            hgrn_g_norm[j], bf(hgrn_w_out[j]), layer=layer)
        x = ffn(x, layer, 1)
    return x
```

```python
import functools

import jax
import jax.numpy as jnp
import numpy as np
from jax import lax
from jax.experimental import pallas as pl
from jax.experimental.pallas import tpu as pltpu

EPS = 1e-6
CONV_WIDTH = 31
HGRN_HEADS = 8
HEAD_DIM = 128

V7X_SUBLANES = 8
V7X_VMEM_BYTES = 64 * 1024 * 1024
VMEM_LIMIT_BYTES = 56 * 1024 * 1024

FFN_ROWS = 1024
FFN_COLS = 256
WEIGHT_STAGE_SLOTS = 2
WEIGHT_STAGE_BYTES = 3 * 512 * 1024
BF16_ROWS_PER_VREG = 2 * V7X_SUBLANES
CONV_ROWS = 256
CONV_HALO = 32
CONV_SUB = 32
CONV_LANES = 512
HGRN_ROWS = 256
HGRN_CHUNK = 64


def _rms(x, gain):
    return x * lax.rsqrt(jnp.mean(x * x, axis=-1, keepdims=True) + EPS) * gain


def _sigmoid(x):
    return 1.0 / (1.0 + jnp.exp(-x))


def _bdot(a, b):
    return jnp.dot(a, b, preferred_element_type=jnp.float32)


def _resident(shape):
    return pl.BlockSpec(shape, lambda *_: (0,) * len(shape), pipeline_mode=pl.Buffered(1))


def _weight_scratch(shape):
    k, n = shape
    cap = WEIGHT_STAGE_BYTES // (4 * n)
    stage_rows = max(r for r in range(BF16_ROWS_PER_VREG, cap + 1, BF16_ROWS_PER_VREG)
                     if k % r == 0)
    return [pltpu.VMEM((k, n), jnp.bfloat16),
            pltpu.VMEM((WEIGHT_STAGE_SLOTS, stage_rows, n), jnp.float32),
            pltpu.SemaphoreType.DMA((WEIGHT_STAGE_SLOTS,))]


def _load_as_bf16(src_hbm, dst_ref, stage_ref, sem):
    chunk = stage_ref.shape[1]
    n = src_hbm.shape[0] // chunk
    assert n * chunk == src_hbm.shape[0] and stage_ref.shape[0] == WEIGHT_STAGE_SLOTS

    def copy(c):
        slot = c % WEIGHT_STAGE_SLOTS
        return pltpu.make_async_copy(src_hbm.at[pl.ds(c * chunk, chunk), :],
                                     stage_ref.at[slot], sem.at[slot])

    copy(0).start()
    for c in range(n):
        if c + 1 < n:
            copy(c + 1).start()
        copy(c).wait()
        dst_ref[c * chunk:(c + 1) * chunk, :] = stage_ref[c % WEIGHT_STAGE_SLOTS].astype(
            jnp.bfloat16)


def _ffn_kernel(x_ref, g_in_ref, g_out_ref, w_in_hbm, w_out_hbm, o_ref, act_ref,
                w_in_ref, stage_in_ref, sem_in, w_out_ref, stage_out_ref, sem_out,
                *, d_ff, layer, which):
    @pl.when(pl.program_id(0) == 0)
    def _():
        _load_as_bf16(w_in_hbm.at[layer, which], w_in_ref, stage_in_ref, sem_in)
        _load_as_bf16(w_out_hbm.at[layer, which], w_out_ref, stage_out_ref, sem_out)

    x = x_ref[...]
    xn = _rms(x, g_in_ref[...]).astype(jnp.bfloat16)
    for c0 in range(0, d_ff, FFN_COLS):
        gate = _bdot(xn, w_in_ref[:, c0:c0 + FFN_COLS])
        up = _bdot(xn, w_in_ref[:, d_ff + c0:d_ff + c0 + FFN_COLS])
        act_ref[:, c0:c0 + FFN_COLS] = (gate * _sigmoid(gate) * up).astype(jnp.bfloat16)
    h = _bdot(act_ref[...], w_out_ref[...])
    o_ref[...] = x + 0.5 * _rms(h, g_out_ref[...])


def _ffn(x2d, g_in, g_out, w_in_all, w_out_all, *, layer, which):
    m, d = x2d.shape
    d_ff = w_out_all.shape[2]
    assert m % FFN_ROWS == 0 and d_ff % FFN_COLS == 0
    row = pl.BlockSpec((FFN_ROWS, d), lambda i: (i, 0))
    hbm = pl.BlockSpec(memory_space=pl.ANY)
    return pl.pallas_call(
        functools.partial(_ffn_kernel, d_ff=d_ff, layer=layer, which=which),
        grid=(m // FFN_ROWS,),
        in_specs=[row, _resident((1, d)), _resident((1, d)), hbm, hbm],
        out_specs=row,
        out_shape=jax.ShapeDtypeStruct((m, d), jnp.float32),
        scratch_shapes=[pltpu.VMEM((FFN_ROWS, d_ff), jnp.bfloat16)]
        + _weight_scratch(w_in_all.shape[2:]) + _weight_scratch(w_out_all.shape[2:]),
        compiler_params=pltpu.CompilerParams(
            dimension_semantics=("arbitrary",), vmem_limit_bytes=VMEM_LIMIT_BYTES),
        name="ffn",
    )(x2d, g_in, g_out, w_in_all, w_out_all)


def _conv_kernel(x_ref, g_pre_ref, g_post_ref, w_in_hbm, b_in_ref, w_dw_ref, b_dw_ref,
                 ln_g_ref, ln_b_ref, w_out_hbm, b_out_ref, o_ref,
                 hext_ref, shift_ref, taps_ref, conv_ref,
                 w_in_ref, stage_in_ref, sem_in, w_out_ref, stage_out_ref, sem_out, *, idx):
    d = x_ref.shape[-1]
    rows = x_ref.shape[1]
    n_ext = CONV_HALO + rows
    first = pl.program_id(1) == 0

    @pl.when(jnp.logical_and(pl.program_id(0) == 0, first))
    def _():
        _load_as_bf16(w_in_hbm.at[idx], w_in_ref, stage_in_ref, sem_in)
        _load_as_bf16(w_out_hbm.at[idx], w_out_ref, stage_out_ref, sem_out)

    @pl.when(first)
    def _():
        hext_ref[0:CONV_HALO, :] = jnp.zeros((CONV_HALO, d), jnp.float32)
        hext_ref[n_ext:n_ext + V7X_SUBLANES, :] = jnp.zeros((V7X_SUBLANES, d), jnp.float32)

    @pl.when(jnp.logical_not(first))
    def _():
        hext_ref[0:CONV_HALO, :] = hext_ref[rows:n_ext, :]

    x = x_ref[0]
    xn = _rms(x, g_pre_ref[...]).astype(jnp.bfloat16)
    y = _bdot(xn, w_in_ref[...]) + b_in_ref[...]
    hext_ref[CONV_HALO:n_ext, :] = y[:, :d] * _sigmoid(y[:, d:])

    sub = lax.broadcasted_iota(jnp.int32, (V7X_SUBLANES, d), 0)
    n_tiles = n_ext // V7X_SUBLANES
    for s in range(1, V7X_SUBLANES):
        rot = [pltpu.roll(hext_ref[i * V7X_SUBLANES:(i + 1) * V7X_SUBLANES, :],
                          V7X_SUBLANES - s, axis=0) for i in range(n_tiles + 1)]
        for i in range(n_tiles):
            shift_ref[s - 1, i * V7X_SUBLANES:(i + 1) * V7X_SUBLANES, :] = jnp.where(
                sub < V7X_SUBLANES - s, rot[i], rot[i + 1])

    for j in range(CONV_WIDTH):
        taps_ref[j] = jnp.broadcast_to(w_dw_ref[j:j + 1, :], (V7X_SUBLANES, d))

    for c0 in range(0, d, CONV_LANES):
        for r0 in range(0, rows, CONV_SUB):
            acc = [jnp.zeros((V7X_SUBLANES, CONV_LANES), jnp.float32)
                   for _ in range(CONV_SUB // V7X_SUBLANES)]
            for j in range(CONV_WIDTH):
                off = CONV_HALO - (CONV_WIDTH - 1) + j
                lo = r0 + off - off % V7X_SUBLANES
                tap = taps_ref[j, :, c0:c0 + CONV_LANES]
                for i in range(len(acc)):
                    r = lo + i * V7X_SUBLANES
                    if off % V7X_SUBLANES == 0:
                        win = hext_ref[r:r + V7X_SUBLANES, c0:c0 + CONV_LANES]
                    else:
                        win = shift_ref[off % V7X_SUBLANES - 1, r:r + V7X_SUBLANES,
                                        c0:c0 + CONV_LANES]
                    acc[i] = acc[i] + tap * win
            for i in range(len(acc)):
                r = r0 + i * V7X_SUBLANES
                conv_ref[r:r + V7X_SUBLANES, c0:c0 + CONV_LANES] = acc[i]

    h = conv_ref[...] + b_dw_ref[...]
    mu = jnp.mean(h, axis=-1, keepdims=True)
    hc = h - mu
    hn = hc * lax.rsqrt(jnp.mean(hc * hc, axis=-1, keepdims=True) + EPS)
    hn = hn * ln_g_ref[...] + ln_b_ref[...]
    act = (hn * _sigmoid(hn)).astype(jnp.bfloat16)
    m = _bdot(act, w_out_ref[...]) + b_out_ref[...]
    o_ref[0] = x + _rms(m, g_post_ref[...])


def _conv_mixer(x, g_pre, g_post, w_in_all, b_in, w_dw, b_dw, ln_g, ln_b, w_out_all, b_out, *, idx):
    bsz, seq, d = x.shape
    assert seq % CONV_ROWS == 0 and CONV_ROWS % CONV_SUB == 0
    assert CONV_HALO >= CONV_WIDTH - 1 and CONV_HALO % V7X_SUBLANES == 0
    n_ext = CONV_HALO + CONV_ROWS
    row = pl.BlockSpec((1, CONV_ROWS, d), lambda b, s: (b, s, 0))
    vec = _resident((1, d))
    hbm = pl.BlockSpec(memory_space=pl.ANY)
    return pl.pallas_call(
        functools.partial(_conv_kernel, idx=idx),
        grid=(bsz, seq // CONV_ROWS),
        in_specs=[row, vec, vec, hbm, _resident((1, 2 * d)),
                  _resident(w_dw.shape), vec, vec, vec, hbm, vec],
        out_specs=row,
        out_shape=jax.ShapeDtypeStruct(x.shape, jnp.float32),
        scratch_shapes=[pltpu.VMEM((n_ext + V7X_SUBLANES, d), jnp.float32),
                        pltpu.VMEM((V7X_SUBLANES - 1, n_ext, d), jnp.float32),
                        pltpu.VMEM((CONV_WIDTH, V7X_SUBLANES, d), jnp.float32),
                        pltpu.VMEM((CONV_ROWS, d), jnp.float32)]
        + _weight_scratch(w_in_all.shape[1:]) + _weight_scratch(w_out_all.shape[1:]),
        compiler_params=pltpu.CompilerParams(
            dimension_semantics=("arbitrary", "arbitrary"), vmem_limit_bytes=VMEM_LIMIT_BYTES),
        name="conv_mixer",
    )(x, g_pre, g_post, w_in_all, b_in, w_dw, b_dw, ln_g, ln_b, w_out_all, b_out)


HGRN_LEVELS = tuple(HGRN_CHUNK >> (i + 1) for i in range(HGRN_CHUNK.bit_length() - 1))
LOG2_E = 1.4426950408889634


def _hgrn_sum_matrix():
    c = HGRN_CHUNK
    t = np.arange(c)[:, None]
    u = np.arange(c)[None, :]
    blocks = [u <= t]
    for h in HGRN_LEVELS:
        mid = (t // (2 * h)) * (2 * h) + h
        upper = t >= mid
        blocks.append(np.where(upper, (u >= mid) & (u <= t), (u > t) & (u < mid)))
    return np.concatenate(blocks, axis=0).astype(np.float32)


def _split3(x):
    hi = x.astype(jnp.bfloat16)
    r1 = x - hi.astype(jnp.float32)
    mid = r1.astype(jnp.bfloat16)
    lo = (r1 - mid.astype(jnp.float32)).astype(jnp.bfloat16)
    return hi, mid, lo


def _dot_nt(a, b):
    return lax.dot_general(a, b, (((1,), (1,)), ((), ())), preferred_element_type=jnp.float32)


def _dot_tn(a, b):
    return lax.dot_general(a, b, (((0,), (0,)), ((), ())), preferred_element_type=jnp.float32)


def _hgrn_kernel(x_ref, g_pre_ref, g_post_ref, w_in_hbm, lb_logits_ref, g_norm_ref, w_out_hbm,
                 sum_ref, o_ref, state_ref,
                 w_in_ref, stage_in_ref, sem_in, w_out_ref, stage_out_ref, sem_out, *, layer, idx):
    d = x_ref.shape[-1]
    rows = x_ref.shape[1]
    c = HGRN_CHUNK
    n_chunks = rows // c
    heads = [slice(hd * HEAD_DIM, (hd + 1) * HEAD_DIM) for hd in range(HGRN_HEADS)]

    @pl.when(jnp.logical_and(pl.program_id(0) == 0, pl.program_id(1) == 0))
    def _():
        _load_as_bf16(w_in_hbm.at[idx], w_in_ref, stage_in_ref, sem_in)
        _load_as_bf16(w_out_hbm.at[idx], w_out_ref, stage_out_ref, sem_out)

    @pl.when(pl.program_id(1) == 0)
    def _():
        state_ref[...] = jnp.zeros(state_ref.shape, jnp.float32)

    logits = lb_logits_ref[...]
    e = jnp.exp(logits - jnp.max(logits, axis=0, keepdims=True))
    p = e / jnp.sum(e, axis=0, keepdims=True)
    cum = p[0:1]
    for l in range(1, layer + 1):
        cum = cum + p[l:l + 1]
    lb = cum - p[0:1]

    x = x_ref[0]
    xn = _rms(x, g_pre_ref[...]).astype(jnp.bfloat16)
    y = _bdot(xn, w_in_ref[...])
    q = y[:, 0:d]
    f = y[:, d:2 * d]
    q = q * _sigmoid(q)
    t = jnp.exp(-jnp.abs(f))
    w = 1.0 + t
    r = 1.0 / w
    a1 = jnp.log(lb)
    a2 = jnp.log1p(-lb) + (jnp.minimum(f, 0.0) - jnp.log(w))
    log_f = jnp.maximum(a1, a2) + jnp.log(1.0 + jnp.exp(-jnp.abs(a1 - a2)))
    k = (1.0 - lb) * jnp.where(f >= 0.0, t * r, r)
    v = y[:, 2 * d:3 * d].astype(jnp.bfloat16)
    lf2 = log_f * LOG2_E

    tt = lax.broadcasted_iota(jnp.int32, (c, c), 0)
    ss = lax.broadcasted_iota(jnp.int32, (c, c), 1)
    masks = [tt == ss]
    for h in HGRN_LEVELS:
        shift = (2 * h).bit_length() - 1
        masks.append(((tt >> shift) == (ss >> shift)) & ((tt & h) != 0) & ((ss & h) == 0))

    def exponents(ci):
        hi, mid, lo = _split3(lf2[ci * c:(ci + 1) * c])
        return _bdot(sum_ref[...], jnp.concatenate([hi, mid, lo], axis=0))

    def chunk_scores(ci, ex):
        qc = q[ci * c:(ci + 1) * c]
        kc = k[ci * c:(ci + 1) * c]
        b = ex[0:c]
        b_last = b[c - 1:c]
        q_dec = (qc * jnp.exp2(b)).astype(jnp.bfloat16)
        k_dec = (kc * jnp.exp2(b_last - b)).astype(jnp.bfloat16)
        q_lv, k_lv = [qc.astype(jnp.bfloat16)], [kc.astype(jnp.bfloat16)]
        for i in range(len(HGRN_LEVELS)):
            fac = jnp.exp2(ex[(1 + i) * c:(2 + i) * c])
            q_lv.append((qc * fac).astype(jnp.bfloat16))
            k_lv.append((kc * fac).astype(jnp.bfloat16))
        scores = []
        for sl in heads:
            acc = jnp.zeros((c, c), jnp.float32)
            for ql, kl, mask in zip(q_lv, k_lv, masks):
                acc = acc + jnp.where(mask, _dot_nt(ql[:, sl], kl[:, sl]), 0.0)
            scores.append(acc.astype(jnp.bfloat16))
        return scores, q_dec, k_dec, jnp.exp2(b_last)

    ex_next = exponents(0)
    per_chunk = []
    for ci in range(n_chunks):
        ex = ex_next
        if ci + 1 < n_chunks:
            ex_next = exponents(ci + 1)
        per_chunk.append(chunk_scores(ci, ex))

    o_inter = []
    for ci, (_, q_dec, k_dec, decay_last) in enumerate(per_chunk):
        vc = v[ci * c:(ci + 1) * c]
        outs = []
        for hd, sl in enumerate(heads):
            state_t = state_ref[hd]
            outs.append(_dot_nt(q_dec[:, sl], state_t.astype(jnp.bfloat16)))
            state_ref[hd] = state_t * decay_last[:, sl] + _dot_tn(vc[:, sl], k_dec[:, sl])
        o_inter.append(outs)

    o_rows = []
    for ci, (scores, _, _, _) in enumerate(per_chunk):
        vc = v[ci * c:(ci + 1) * c]
        o_rows.append(jnp.concatenate(
            [o_inter[ci][hd] + _bdot(scores[hd], vc[:, sl]) for hd, sl in enumerate(heads)],
            axis=-1))
    o = jnp.concatenate(o_rows, axis=0)

    gate = y[:, 3 * d:4 * d]
    gate = gate * _sigmoid(gate)
    parts = []
    for sl in heads:
        oh = o[:, sl]
        parts.append(oh * lax.rsqrt(jnp.mean(oh * oh, axis=-1, keepdims=True) + EPS))
    on = jnp.concatenate(parts, axis=-1) * g_norm_ref[...] * gate
    m = _bdot(on.astype(jnp.bfloat16), w_out_ref[...])
    o_ref[0] = x + _rms(m, g_post_ref[...])


def _hgrn_mixer(x, g_pre, g_post, w_in_all, lb_logits, g_norm, w_out_all, *, layer, idx):
    bsz, seq, d = x.shape
    assert seq % HGRN_ROWS == 0 and HGRN_ROWS % HGRN_CHUNK == 0
    assert d == HGRN_HEADS * HEAD_DIM and w_in_all.shape[1:] == (d, 4 * d)
    sum_mat = _hgrn_sum_matrix()
    sum3 = jnp.asarray(np.concatenate([sum_mat] * 3, axis=1), jnp.bfloat16)
    g_norm_row = jnp.tile(g_norm.reshape(1, HEAD_DIM), (1, HGRN_HEADS))
    row = pl.BlockSpec((1, HGRN_ROWS, d), lambda b, s: (b, s, 0))
    vec = _resident((1, d))
    hbm = pl.BlockSpec(memory_space=pl.ANY)
    return pl.pallas_call(
        functools.partial(_hgrn_kernel, layer=layer, idx=idx),
        grid=(bsz, seq // HGRN_ROWS),
        in_specs=[row, vec, vec, hbm, _resident(lb_logits.shape), vec, hbm,
                  _resident(sum3.shape)],
        out_specs=row,
        out_shape=jax.ShapeDtypeStruct(x.shape, jnp.float32),
        scratch_shapes=[pltpu.VMEM((HGRN_HEADS, HEAD_DIM, HEAD_DIM), jnp.float32)]
        + _weight_scratch(w_in_all.shape[1:]) + _weight_scratch(w_out_all.shape[1:]),
        compiler_params=pltpu.CompilerParams(
            dimension_semantics=("arbitrary", "arbitrary"), vmem_limit_bytes=VMEM_LIMIT_BYTES),
        name="hgrn_mixer",
    )(x, g_pre, g_post, w_in_all, lb_logits, g_norm_row, w_out_all, sum3)


def kernel(x, norm_gains, ffn_w_in, ffn_w_out, conv_w_in, conv_b_in, conv_w_dw, conv_b_dw,
           conv_ln_g, conv_ln_b, conv_w_out, conv_b_out, hgrn_w_in, hgrn_lb_logits,
           hgrn_g_norm, hgrn_w_out):
    bsz, seq, d = x.shape
    depth = norm_gains.shape[0]
    vec = lambda v: v.reshape(1, -1).astype(jnp.float32)

    def ffn(x, layer, which):
        g = norm_gains[layer]
        y = _ffn(x.reshape(bsz * seq, d), vec(g[4 * which]), vec(g[4 * which + 1]),
                 ffn_w_in, ffn_w_out, layer=layer, which=which)
        return y.reshape(bsz, seq, d)

    for layer in range(depth):
        g = norm_gains[layer]
        x = ffn(x, layer, 0)
        j = layer // 2
        if layer % 2 == 0:
            x = _conv_mixer(x, vec(g[2]), vec(g[3]), conv_w_in, vec(conv_b_in[j]),
                            conv_w_dw[j], vec(conv_b_dw[j]), vec(conv_ln_g[j]),
                            vec(conv_ln_b[j]), conv_w_out, vec(conv_b_out[j]), idx=j)
        else:
            x = _hgrn_mixer(x, vec(g[2]), vec(g[3]), hgrn_w_in, hgrn_lb_logits,
                            hgrn_g_norm[j], hgrn_w_out, layer=layer, idx=j)
        x = ffn(x, layer, 1)
    return x
```

```python
import functools

import jax
import jax.numpy as jnp
import numpy as np
from jax import lax
from jax.experimental import pallas as pl
from jax.experimental.pallas import tpu as pltpu

EPS = 1e-6
CONV_WIDTH = 31
HGRN_HEADS = 8
HEAD_DIM = 128

V7X_SUBLANES = 8
V7X_VMEM_BYTES = 64 * 1024 * 1024
VMEM_LIMIT_BYTES = 56 * 1024 * 1024

FFN_ROWS = 1024
FFN_COLS = 256
FFN_STAGE_COLS = 512
WEIGHT_STAGE_SLOTS = 2
WEIGHT_STAGE_BYTES = 768 * 1024
BF16_ROWS_PER_VREG = 2 * V7X_SUBLANES
CONV_ROWS = 256
CONV_HALO = 32
CONV_SUB = 64
CONV_LANES = 256
HGRN_ROWS = 256
HGRN_SUB = 128
HGRN_CHUNK = 64


def _rms(x, gain):
    return x * lax.rsqrt(jnp.mean(x * x, axis=-1, keepdims=True) + EPS) * gain


def _sigmoid(x):
    return 1.0 / (1.0 + jnp.exp(-x))


def _bdot(a, b):
    return jnp.dot(a, b, preferred_element_type=jnp.float32)


def _resident(shape):
    return pl.BlockSpec(shape, lambda *_: (0,) * len(shape), pipeline_mode=pl.Buffered(1))


def _weight_scratch(shape, col_block=None):
    k, n = shape
    cap = WEIGHT_STAGE_BYTES // (4 * n)
    stage_rows = max(r for r in range(BF16_ROWS_PER_VREG, cap + 1, BF16_ROWS_PER_VREG)
                     if k % r == 0)
    resident = (k, n) if col_block is None else (n // col_block, k, col_block)
    return [pltpu.VMEM(resident, jnp.bfloat16),
            pltpu.VMEM((WEIGHT_STAGE_SLOTS, stage_rows, n), jnp.float32),
            pltpu.SemaphoreType.DMA((WEIGHT_STAGE_SLOTS,))]


def _load_as_bf16(src_hbm, dst_ref, stage_ref, sem):
    chunk = stage_ref.shape[1]
    n = src_hbm.shape[0] // chunk
    assert n * chunk == src_hbm.shape[0] and stage_ref.shape[0] == WEIGHT_STAGE_SLOTS

    def copy(c):
        slot = c % WEIGHT_STAGE_SLOTS
        return pltpu.make_async_copy(src_hbm.at[pl.ds(c * chunk, chunk), :],
                                     stage_ref.at[slot], sem.at[slot])

    copy(0).start()
    for c in range(n):
        if c + 1 < n:
            copy(c + 1).start()
        copy(c).wait()
        staged = stage_ref[c % WEIGHT_STAGE_SLOTS].astype(jnp.bfloat16)
        if len(dst_ref.shape) == 2:
            dst_ref[c * chunk:(c + 1) * chunk, :] = staged
        else:
            width = dst_ref.shape[2]
            for b in range(dst_ref.shape[0]):
                dst_ref[b, c * chunk:(c + 1) * chunk, :] = staged[:, b * width:(b + 1) * width]


def _ffn_block(x, xn, g_out_ref, w_in_ref, w_out_ref, act_ref, d_ff):
    for c0 in range(0, d_ff, FFN_COLS):
        gate = _bdot(xn, w_in_ref[:, c0:c0 + FFN_COLS])
        up = _bdot(xn, w_in_ref[:, d_ff + c0:d_ff + c0 + FFN_COLS])
        act_ref[:, c0:c0 + FFN_COLS] = (gate * _sigmoid(gate) * up).astype(jnp.bfloat16)
    h = _bdot(act_ref[...], w_out_ref[...])
    return x + 0.5 * _rms(h, g_out_ref[...])


def _ffn_stage_products(b, xn_ref, w_in_ref, gu_ref):
    gu_ref[b] = _bdot(xn_ref[...], w_in_ref[b])


def _ffn_from_products(x, g_out_ref, w_out_ref, gu_ref, act_ref, d_ff):
    width = gu_ref.shape[2]

    def cols(flat):
        return gu_ref[flat // width, :, flat % width:flat % width + FFN_COLS]

    for c0 in range(0, d_ff, FFN_COLS):
        gate = cols(c0)
        up = cols(d_ff + c0)
        act_ref[:, c0:c0 + FFN_COLS] = (gate * _sigmoid(gate) * up).astype(jnp.bfloat16)
    h = _bdot(act_ref[...], w_out_ref[...])
    return x + 0.5 * _rms(h, g_out_ref[...])


def _ffn_kernel(x_ref, g_in_ref, g_out_ref, w_in_hbm, w_out_hbm, o_ref, act_ref,
                w_in_ref, stage_in_ref, sem_in, w_out_ref, stage_out_ref, sem_out,
                *, d_ff, layer, which):
    @pl.when(pl.program_id(0) == 0)
    def _():
        _load_as_bf16(w_in_hbm.at[layer, which], w_in_ref, stage_in_ref, sem_in)
        _load_as_bf16(w_out_hbm.at[layer, which], w_out_ref, stage_out_ref, sem_out)

    x = x_ref[...]
    xn = _rms(x, g_in_ref[...]).astype(jnp.bfloat16)
    o_ref[...] = _ffn_block(x, xn, g_out_ref, w_in_ref, w_out_ref, act_ref, d_ff)


def _ffn(x2d, g_in, g_out, w_in_all, w_out_all, *, layer, which):
    m, d = x2d.shape
    d_ff = w_out_all.shape[2]
    assert m % FFN_ROWS == 0 and d_ff % FFN_COLS == 0
    row = pl.BlockSpec((FFN_ROWS, d), lambda i: (i, 0))
    hbm = pl.BlockSpec(memory_space=pl.ANY)
    return pl.pallas_call(
        functools.partial(_ffn_kernel, d_ff=d_ff, layer=layer, which=which),
        grid=(m // FFN_ROWS,),
        in_specs=[row, _resident((1, d)), _resident((1, d)), hbm, hbm],
        out_specs=row,
        out_shape=jax.ShapeDtypeStruct((m, d), jnp.float32),
        scratch_shapes=[pltpu.VMEM((FFN_ROWS, d_ff), jnp.bfloat16)]
        + _weight_scratch(w_in_all.shape[2:]) + _weight_scratch(w_out_all.shape[2:]),
        compiler_params=pltpu.CompilerParams(
            dimension_semantics=("arbitrary",), vmem_limit_bytes=VMEM_LIMIT_BYTES),
        name="ffn",
    )(x2d, g_in, g_out, w_in_all, w_out_all)


def _conv_ffn_kernel(x_ref, g_pre_ref, g_post_ref, cw_in_hbm, b_in_ref, w_dw_ref, b_dw_ref,
                     ln_g_ref, ln_b_ref, cw_out_hbm, b_out_ref,
                     fg_in_ref, fg_out_ref, fw_in_hbm, fw_out_hbm, o_ref,
                     hext_ref, shift_ref, taps_ref, conv_ref, mid_ref, xn_ref, gu_ref, act_ref,
                     cw_in_ref, cstage_in_ref, csem_in, cw_out_ref, cstage_out_ref, csem_out,
                     fw_in_ref, fstage_in_ref, fsem_in, fw_out_ref, fstage_out_ref, fsem_out,
                     *, idx, layer, which, tiles_per_seq, d_ff):
    rows, d = x_ref.shape
    n_ext = CONV_HALO + rows
    step = pl.program_id(0)
    first = step % tiles_per_seq == 0

    @pl.when(step == 0)
    def _():
        _load_as_bf16(cw_in_hbm.at[idx], cw_in_ref, cstage_in_ref, csem_in)
        _load_as_bf16(cw_out_hbm.at[idx], cw_out_ref, cstage_out_ref, csem_out)
        _load_as_bf16(fw_in_hbm.at[layer, which], fw_in_ref, fstage_in_ref, fsem_in)
        _load_as_bf16(fw_out_hbm.at[layer, which], fw_out_ref, fstage_out_ref, fsem_out)
        mid_ref[...] = jnp.zeros(mid_ref.shape, jnp.float32)
        xn_ref[...] = jnp.zeros(xn_ref.shape, jnp.bfloat16)

    @pl.when(first)
    def _():
        hext_ref[0:CONV_HALO, :] = jnp.zeros((CONV_HALO, d), jnp.float32)
        hext_ref[n_ext:n_ext + V7X_SUBLANES, :] = jnp.zeros((V7X_SUBLANES, d), jnp.float32)

    @pl.when(jnp.logical_not(first))
    def _():
        hext_ref[0:CONV_HALO, :] = hext_ref[rows:n_ext, :]

    mid_prev = mid_ref[...]
    x = x_ref[...]
    xn = _rms(x, g_pre_ref[...]).astype(jnp.bfloat16)
    y = _bdot(xn, cw_in_ref[...]) + b_in_ref[...]
    hext_ref[CONV_HALO:n_ext, :] = y[:, :d] * _sigmoid(y[:, d:])

    n_blocks = gu_ref.shape[0]
    n_loop = rows // CONV_SUB
    per_iter = n_blocks // n_loop
    n_early = n_blocks - n_loop * per_iter
    for b in range(n_early):
        _ffn_stage_products(b, xn_ref, fw_in_ref, gu_ref)

    sub = lax.broadcasted_iota(jnp.int32, (V7X_SUBLANES, d), 0)
    n_tiles = n_ext // V7X_SUBLANES
    for s in range(1, V7X_SUBLANES):
        rot = [pltpu.roll(hext_ref[i * V7X_SUBLANES:(i + 1) * V7X_SUBLANES, :],
                          V7X_SUBLANES - s, axis=0) for i in range(n_tiles + 1)]
        for i in range(n_tiles):
            shift_ref[s - 1, i * V7X_SUBLANES:(i + 1) * V7X_SUBLANES, :] = jnp.where(
                sub < V7X_SUBLANES - s, rot[i], rot[i + 1])

    for j in range(CONV_WIDTH):
        taps_ref[j] = jnp.broadcast_to(w_dw_ref[j:j + 1, :], (V7X_SUBLANES, d))

    offsets = [CONV_HALO - (CONV_WIDTH - 1) + j for j in range(CONV_WIDTH)]
    n_acc = CONV_SUB // V7X_SUBLANES

    def tap_rows(it, carry):
        for b in range(per_iter):
            _ffn_stage_products(n_early + it * per_iter + b, xn_ref, fw_in_ref, gu_ref)
        r0 = pl.multiple_of(it * CONV_SUB, CONV_SUB)
        for c0 in range(0, d, CONV_LANES):
            lanes = slice(c0, c0 + CONV_LANES)
            acc = [jnp.zeros((V7X_SUBLANES, CONV_LANES), jnp.float32) for _ in range(n_acc)]
            for phase in range(V7X_SUBLANES):
                taps = [(j, off // V7X_SUBLANES) for j, off in enumerate(offsets)
                        if off % V7X_SUBLANES == phase]
                first_tile = min(a for _, a in taps)
                last_tile = max(a for _, a in taps) + n_acc
                win = {}
                for a in range(first_tile, last_tile):
                    r = pl.ds(r0 + a * V7X_SUBLANES, V7X_SUBLANES)
                    win[a] = hext_ref[r, lanes] if phase == 0 else shift_ref[phase - 1, r, lanes]
                for j, a in taps:
                    tap = taps_ref[j, :, lanes]
                    for i in range(n_acc):
                        acc[i] = acc[i] + tap * win[a + i]
            for i in range(n_acc):
                conv_ref[pl.ds(r0 + i * V7X_SUBLANES, V7X_SUBLANES), lanes] = acc[i]
        return carry

    lax.fori_loop(0, n_loop, tap_rows, 0)

    o_ref[...] = _ffn_from_products(mid_prev, fg_out_ref, fw_out_ref, gu_ref, act_ref, d_ff)

    h = conv_ref[...] + b_dw_ref[...]
    mu = jnp.mean(h, axis=-1, keepdims=True)
    hc = h - mu
    hn = hc * lax.rsqrt(jnp.mean(hc * hc, axis=-1, keepdims=True) + EPS)
    hn = hn * ln_g_ref[...] + ln_b_ref[...]
    act = (hn * _sigmoid(hn)).astype(jnp.bfloat16)
    m = _bdot(act, cw_out_ref[...]) + b_out_ref[...]
    mid = x + _rms(m, g_post_ref[...])
    mid_ref[...] = mid
    xn_ref[...] = _rms(mid, fg_in_ref[...]).astype(jnp.bfloat16)


def _conv_ffn(x2d, seq, g_pre, g_post, cw_in_all, b_in, w_dw, b_dw, ln_g, ln_b, cw_out_all, b_out,
              fg_in, fg_out, fw_in_all, fw_out_all, *, idx, layer, which):
    m, d = x2d.shape
    d_ff = fw_out_all.shape[2]
    rows = CONV_ROWS
    assert seq % rows == 0 and m % seq == 0 and rows % CONV_SUB == 0 and d % CONV_LANES == 0
    assert CONV_HALO >= CONV_WIDTH - 1 and CONV_HALO % V7X_SUBLANES == 0
    n_ext = CONV_HALO + rows
    n_tiles = m // rows
    vec = _resident((1, d))
    hbm = pl.BlockSpec(memory_space=pl.ANY)
    return pl.pallas_call(
        functools.partial(_conv_ffn_kernel, idx=idx, layer=layer, which=which,
                          tiles_per_seq=seq // rows, d_ff=d_ff),
        grid=(n_tiles + 1,),
        in_specs=[pl.BlockSpec((rows, d), lambda i: (jnp.minimum(i, n_tiles - 1), 0)),
                  vec, vec, hbm, _resident((1, 2 * d)), _resident(w_dw.shape), vec, vec, vec,
                  hbm, vec, vec, vec, hbm, hbm],
        out_specs=pl.BlockSpec((rows, d), lambda i: (jnp.maximum(i - 1, 0), 0)),
        out_shape=jax.ShapeDtypeStruct((m, d), jnp.float32),
        scratch_shapes=[pltpu.VMEM((n_ext + V7X_SUBLANES, d), jnp.float32),
                        pltpu.VMEM((V7X_SUBLANES - 1, n_ext, d), jnp.float32),
                        pltpu.VMEM((CONV_WIDTH, V7X_SUBLANES, d), jnp.float32),
                        pltpu.VMEM((rows, d), jnp.float32),
                        pltpu.VMEM((rows, d), jnp.float32),
                        pltpu.VMEM((rows, d), jnp.bfloat16),
                        pltpu.VMEM((2 * d_ff // FFN_STAGE_COLS, rows, FFN_STAGE_COLS),
                                   jnp.float32),
                        pltpu.VMEM((rows, d_ff), jnp.bfloat16)]
        + _weight_scratch(cw_in_all.shape[1:]) + _weight_scratch(cw_out_all.shape[1:])
        + _weight_scratch(fw_in_all.shape[2:], col_block=FFN_STAGE_COLS)
        + _weight_scratch(fw_out_all.shape[2:]),
        compiler_params=pltpu.CompilerParams(
            dimension_semantics=("arbitrary",), vmem_limit_bytes=VMEM_LIMIT_BYTES),
        name="conv_ffn",
    )(x2d, g_pre, g_post, cw_in_all, b_in, w_dw, b_dw, ln_g, ln_b, cw_out_all, b_out,
      fg_in, fg_out, fw_in_all, fw_out_all)


HGRN_LEVELS = tuple(HGRN_CHUNK >> (i + 1) for i in range(HGRN_CHUNK.bit_length() - 1))
HGRN_MXU_LEVELS = tuple(h for h in HGRN_LEVELS if 2 * h < V7X_SUBLANES)
LOG2_E = 1.4426950408889634


def _hgrn_sum_matrix():
    c = HGRN_CHUNK
    t = np.arange(c)[:, None]
    u = np.arange(c)[None, :]
    blocks = [u <= t]
    for h in HGRN_MXU_LEVELS:
        mid = (t // (2 * h)) * (2 * h) + h
        upper = t >= mid
        blocks.append(np.where(upper, (u >= mid) & (u <= t), (u > t) & (u < mid)))
    return np.concatenate(blocks, axis=0).astype(np.float32)


def _split3(x):
    hi = x.astype(jnp.bfloat16)
    r1 = x - hi.astype(jnp.float32)
    mid = r1.astype(jnp.bfloat16)
    lo = (r1 - mid.astype(jnp.float32)).astype(jnp.bfloat16)
    return hi, mid, lo


def _dot_nt(a, b):
    return lax.dot_general(a, b, (((1,), (1,)), ((), ())), preferred_element_type=jnp.float32)


def _dot_tn(a, b):
    return lax.dot_general(a, b, (((0,), (0,)), ((), ())), preferred_element_type=jnp.float32)


def _hgrn_kernel(x_ref, g_pre_ref, g_post_ref, w_in_hbm, lb_logits_ref, g_norm_ref, w_out_hbm,
                 sum_ref, o_ref, state_ref,
                 w_in_ref, stage_in_ref, sem_in, w_out_ref, stage_out_ref, sem_out, *, layer, idx):
    d = x_ref.shape[-1]
    rows = x_ref.shape[1]
    c = HGRN_CHUNK
    sub = HGRN_SUB
    n_sub = rows // sub
    chunks_per_sub = sub // c
    heads = [slice(hd * HEAD_DIM, (hd + 1) * HEAD_DIM) for hd in range(HGRN_HEADS)]

    @pl.when(jnp.logical_and(pl.program_id(0) == 0, pl.program_id(1) == 0))
    def _():
        _load_as_bf16(w_in_hbm.at[idx], w_in_ref, stage_in_ref, sem_in)
        _load_as_bf16(w_out_hbm.at[idx], w_out_ref, stage_out_ref, sem_out)

    @pl.when(pl.program_id(1) == 0)
    def _():
        state_ref[...] = jnp.zeros(state_ref.shape, jnp.float32)

    logits = lb_logits_ref[...]
    e = jnp.exp(logits - jnp.max(logits, axis=0, keepdims=True))
    p = e / jnp.sum(e, axis=0, keepdims=True)
    cum = p[0:1]
    for l in range(1, layer + 1):
        cum = cum + p[l:l + 1]
    lb = cum - p[0:1]

    tt = lax.broadcasted_iota(jnp.int32, (c, c), 0)
    ss = lax.broadcasted_iota(jnp.int32, (c, c), 1)
    diag_mask = tt == ss
    masks = []
    for h in HGRN_LEVELS:
        shift = (2 * h).bit_length() - 1
        masks.append(((tt >> shift) == (ss >> shift)) & ((tt & h) != 0) & ((ss & h) == 0))

    def project(si):
        x = x_ref[0, si * sub:(si + 1) * sub, :]
        xn = _rms(x, g_pre_ref[...]).astype(jnp.bfloat16)
        y = _bdot(xn, w_in_ref[...])
        q = y[:, 0:d]
        f = y[:, d:2 * d]
        q = q * _sigmoid(q)
        t = jnp.exp(-jnp.abs(f))
        w = 1.0 + t
        r = 1.0 / w
        a1 = jnp.log(lb)
        a2 = jnp.log1p(-lb) + (jnp.minimum(f, 0.0) - jnp.log(w))
        log_f = jnp.maximum(a1, a2) + jnp.log(1.0 + jnp.exp(-jnp.abs(a1 - a2)))
        k = (1.0 - lb) * jnp.where(f >= 0.0, t * r, r)
        v = y[:, 2 * d:3 * d].astype(jnp.bfloat16)
        return dict(x=x, q=q, k=k, v=v, lf2=log_f * LOG2_E, gate=y[:, 3 * d:4 * d])

    def exponents(t, ci):
        hi, mid, lo = _split3(t["lf2"][ci * c:(ci + 1) * c])
        return _bdot(sum_ref[...], jnp.concatenate([hi, mid, lo], axis=0))

    def chunk_scores(t, ci, ex):
        qc = t["q"][ci * c:(ci + 1) * c]
        kc = t["k"][ci * c:(ci + 1) * c]
        b = ex[0:c]
        b_last = b[c - 1:c]
        q_dec = (qc * jnp.exp2(b)).astype(jnp.bfloat16)
        k_dec = (kc * jnp.exp2(b_last - b)).astype(jnp.bfloat16)
        q_lv, k_lv = [], []
        mxu_block = 1
        for h in HGRN_LEVELS:
            if h in HGRN_MXU_LEVELS:
                expo = ex[mxu_block * c:(mxu_block + 1) * c]
                mxu_block += 1
            else:
                ref = jnp.concatenate(
                    [jnp.broadcast_to(b[g + h - 1:g + h], (2 * h, d)) for g in range(0, c, 2 * h)],
                    axis=0)
                expo = -jnp.abs(b - ref)
            fac = jnp.exp2(expo)
            q_lv.append((qc * fac).astype(jnp.bfloat16))
            k_lv.append((kc * fac).astype(jnp.bfloat16))
        qk = qc * kc
        scores = []
        for sl in heads:
            acc = jnp.where(diag_mask, jnp.sum(qk[:, sl], axis=-1, keepdims=True), 0.0)
            for ql, kl, mask in zip(q_lv, k_lv, masks):
                acc = acc + jnp.where(mask, _dot_nt(ql[:, sl], kl[:, sl]), 0.0)
            scores.append(acc.astype(jnp.bfloat16))
        return scores, q_dec, k_dec, jnp.exp2(b_last)

    def state_path(t, ci, q_dec, k_dec, decay_last):
        vc = t["v"][ci * c:(ci + 1) * c]
        outs = []
        for hd, sl in enumerate(heads):
            state_t = state_ref[hd]
            outs.append(_dot_nt(q_dec[:, sl], state_t.astype(jnp.bfloat16)))
            state_ref[hd] = state_t * decay_last[:, sl] + _dot_tn(vc[:, sl], k_dec[:, sl])
        return outs

    def finish(si, t, o):
        gate = t["gate"] * _sigmoid(t["gate"])
        parts = []
        for sl in heads:
            oh = o[:, sl]
            parts.append(oh * lax.rsqrt(jnp.mean(oh * oh, axis=-1, keepdims=True) + EPS))
        on = jnp.concatenate(parts, axis=-1) * g_norm_ref[...] * gate
        m = _bdot(on.astype(jnp.bfloat16), w_out_ref[...])
        o_ref[0, si * sub:(si + 1) * sub, :] = t["x"] + _rms(m, g_post_ref[...])

    tiles = [project(si) for si in range(n_sub)]
    pending = None
    for si, t in enumerate(tiles):
        exs = [exponents(t, ci) for ci in range(chunks_per_sub)]
        if pending is not None:
            finish(*pending)
        per_chunk = [chunk_scores(t, ci, exs[ci]) for ci in range(chunks_per_sub)]
        o_inter = [state_path(t, ci, *pc[1:]) for ci, pc in enumerate(per_chunk)]
        o_rows = []
        for ci, pc in enumerate(per_chunk):
            vc = t["v"][ci * c:(ci + 1) * c]
            o_rows.append(jnp.concatenate(
                [o_inter[ci][hd] + _bdot(pc[0][hd], vc[:, sl]) for hd, sl in enumerate(heads)],
                axis=-1))
        pending = (si, t, jnp.concatenate(o_rows, axis=0))
    finish(*pending)


def _hgrn_mixer(x, g_pre, g_post, w_in_all, lb_logits, g_norm, w_out_all, *, layer, idx):
    bsz, seq, d = x.shape
    assert seq % HGRN_ROWS == 0 and HGRN_ROWS % HGRN_CHUNK == 0
    assert d == HGRN_HEADS * HEAD_DIM and w_in_all.shape[1:] == (d, 4 * d)
    sum_mat = _hgrn_sum_matrix()
    sum3 = jnp.asarray(np.concatenate([sum_mat] * 3, axis=1), jnp.bfloat16)
    g_norm_row = jnp.tile(g_norm.reshape(1, HEAD_DIM), (1, HGRN_HEADS))
    row = pl.BlockSpec((1, HGRN_ROWS, d), lambda b, s: (b, s, 0))
    vec = _resident((1, d))
    hbm = pl.BlockSpec(memory_space=pl.ANY)
    return pl.pallas_call(
        functools.partial(_hgrn_kernel, layer=layer, idx=idx),
        grid=(bsz, seq // HGRN_ROWS),
        in_specs=[row, vec, vec, hbm, _resident(lb_logits.shape), vec, hbm,
                  _resident(sum3.shape)],
        out_specs=row,
        out_shape=jax.ShapeDtypeStruct(x.shape, jnp.float32),
        scratch_shapes=[pltpu.VMEM((HGRN_HEADS, HEAD_DIM, HEAD_DIM), jnp.float32)]
        + _weight_scratch(w_in_all.shape[1:]) + _weight_scratch(w_out_all.shape[1:]),
        compiler_params=pltpu.CompilerParams(
            dimension_semantics=("arbitrary", "arbitrary"), vmem_limit_bytes=VMEM_LIMIT_BYTES),
        name="hgrn_mixer",
    )(x, g_pre, g_post, w_in_all, lb_logits, g_norm_row, w_out_all, sum3)


def kernel(x, norm_gains, ffn_w_in, ffn_w_out, conv_w_in, conv_b_in, conv_w_dw, conv_b_dw,
           conv_ln_g, conv_ln_b, conv_w_out, conv_b_out, hgrn_w_in, hgrn_lb_logits,
           hgrn_g_norm, hgrn_w_out):
    bsz, seq, d = x.shape
    depth = norm_gains.shape[0]
    vec = lambda v: v.reshape(1, -1).astype(jnp.float32)

    def ffn(x, layer, which):
        g = norm_gains[layer]
        y = _ffn(x.reshape(bsz * seq, d), vec(g[4 * which]), vec(g[4 * which + 1]),
                 ffn_w_in, ffn_w_out, layer=layer, which=which)
        return y.reshape(bsz, seq, d)

    for layer in range(depth):
        g = norm_gains[layer]
        x = ffn(x, layer, 0)
        j = layer // 2
        if layer % 2 == 0:
            x = _conv_ffn(x.reshape(bsz * seq, d), seq, vec(g[2]), vec(g[3]), conv_w_in,
                          vec(conv_b_in[j]), conv_w_dw[j], vec(conv_b_dw[j]), vec(conv_ln_g[j]),
                          vec(conv_ln_b[j]), conv_w_out, vec(conv_b_out[j]),
                          vec(g[4]), vec(g[5]), ffn_w_in, ffn_w_out,
                          idx=j, layer=layer, which=1).reshape(bsz, seq, d)
        else:
            x = _hgrn_mixer(x, vec(g[2]), vec(g[3]), hgrn_w_in, hgrn_lb_logits,
                            hgrn_g_norm[j], hgrn_w_out, layer=layer, idx=j)
            x = ffn(x, layer, 1)
    return x
```

```python
import functools

import jax
import jax.numpy as jnp
import numpy as np
from jax import lax
from jax.experimental import pallas as pl
from jax.experimental.pallas import tpu as pltpu

EPS = 1e-6
CONV_WIDTH = 31
HGRN_HEADS = 8
HEAD_DIM = 128

V7X_SUBLANES = 8
V7X_VMEM_BYTES = 64 * 1024 * 1024
VMEM_LIMIT_BYTES = 56 * 1024 * 1024

FFN_ROWS = 1024
FFN_COLS = 256
WEIGHT_STAGE_SLOTS = 2
WEIGHT_STAGE_BYTES = 3 * 512 * 1024
BF16_ROWS_PER_VREG = 2 * V7X_SUBLANES
CONV_ROWS = 256
CONV_HALO = 32
CONV_SUB = 32
CONV_LANES = 512
HGRN_ROWS = 512
HGRN_SUB = 512
HGRN_CHUNK = 64
HGRN_GROUP = 2


def _rms(x, gain):
    return x * lax.rsqrt(jnp.mean(x * x, axis=-1, keepdims=True) + EPS) * gain


def _sigmoid(x):
    return 1.0 / (1.0 + jnp.exp(-x))


def _bdot(a, b):
    return jnp.dot(a, b, preferred_element_type=jnp.float32)


def _resident(shape):
    return pl.BlockSpec(shape, lambda *_: (0,) * len(shape), pipeline_mode=pl.Buffered(1))


def _weight_scratch(shape):
    k, n = shape
    cap = WEIGHT_STAGE_BYTES // (4 * n)
    stage_rows = max(r for r in range(BF16_ROWS_PER_VREG, cap + 1, BF16_ROWS_PER_VREG)
                     if k % r == 0)
    return [pltpu.VMEM((k, n), jnp.bfloat16),
            pltpu.VMEM((WEIGHT_STAGE_SLOTS, stage_rows, n), jnp.float32),
            pltpu.SemaphoreType.DMA((WEIGHT_STAGE_SLOTS,))]


def _load_as_bf16(src_hbm, dst_ref, stage_ref, sem):
    chunk = stage_ref.shape[1]
    n = src_hbm.shape[0] // chunk
    assert n * chunk == src_hbm.shape[0] and stage_ref.shape[0] == WEIGHT_STAGE_SLOTS

    def copy(c):
        slot = c % WEIGHT_STAGE_SLOTS
        return pltpu.make_async_copy(src_hbm.at[pl.ds(c * chunk, chunk), :],
                                     stage_ref.at[slot], sem.at[slot])

    copy(0).start()
    for c in range(n):
        if c + 1 < n:
            copy(c + 1).start()
        copy(c).wait()
        dst_ref[c * chunk:(c + 1) * chunk, :] = stage_ref[c % WEIGHT_STAGE_SLOTS].astype(
            jnp.bfloat16)


def _ffn_kernel(x_ref, g_in_ref, g_out_ref, w_in_hbm, w_out_hbm, o_ref, act_ref,
                w_in_ref, stage_in_ref, sem_in, w_out_ref, stage_out_ref, sem_out,
                *, d_ff, layer, which):
    @pl.when(pl.program_id(0) == 0)
    def _():
        _load_as_bf16(w_in_hbm.at[layer, which], w_in_ref, stage_in_ref, sem_in)
        _load_as_bf16(w_out_hbm.at[layer, which], w_out_ref, stage_out_ref, sem_out)

    x = x_ref[...]
    xn = _rms(x, g_in_ref[...]).astype(jnp.bfloat16)
    for c0 in range(0, d_ff, FFN_COLS):
        gate = _bdot(xn, w_in_ref[:, c0:c0 + FFN_COLS])
        up = _bdot(xn, w_in_ref[:, d_ff + c0:d_ff + c0 + FFN_COLS])
        act_ref[:, c0:c0 + FFN_COLS] = (gate * _sigmoid(gate) * up).astype(jnp.bfloat16)
    h = _bdot(act_ref[...], w_out_ref[...])
    o_ref[...] = x + 0.5 * _rms(h, g_out_ref[...])


def _ffn(x2d, g_in, g_out, w_in_all, w_out_all, *, layer, which):
    m, d = x2d.shape
    d_ff = w_out_all.shape[2]
    assert m % FFN_ROWS == 0 and d_ff % FFN_COLS == 0
    row = pl.BlockSpec((FFN_ROWS, d), lambda i: (i, 0))
    hbm = pl.BlockSpec(memory_space=pl.ANY)
    return pl.pallas_call(
        functools.partial(_ffn_kernel, d_ff=d_ff, layer=layer, which=which),
        grid=(m // FFN_ROWS,),
        in_specs=[row, _resident((1, d)), _resident((1, d)), hbm, hbm],
        out_specs=row,
        out_shape=jax.ShapeDtypeStruct((m, d), jnp.float32),
        scratch_shapes=[pltpu.VMEM((FFN_ROWS, d_ff), jnp.bfloat16)]
        + _weight_scratch(w_in_all.shape[2:]) + _weight_scratch(w_out_all.shape[2:]),
        compiler_params=pltpu.CompilerParams(
            dimension_semantics=("arbitrary",), vmem_limit_bytes=VMEM_LIMIT_BYTES),
        name="ffn",
    )(x2d, g_in, g_out, w_in_all, w_out_all)


def _conv_kernel(x_ref, g_pre_ref, g_post_ref, w_in_hbm, b_in_ref, w_dw_ref, b_dw_ref,
                 ln_g_ref, ln_b_ref, w_out_hbm, b_out_ref, o_ref,
                 hext_ref, shift_ref, taps_ref, conv_ref,
                 w_in_ref, stage_in_ref, sem_in, w_out_ref, stage_out_ref, sem_out, *, idx):
    d = x_ref.shape[-1]
    rows = x_ref.shape[1]
    n_ext = CONV_HALO + rows
    first = pl.program_id(1) == 0

    @pl.when(jnp.logical_and(pl.program_id(0) == 0, first))
    def _():
        _load_as_bf16(w_in_hbm.at[idx], w_in_ref, stage_in_ref, sem_in)
        _load_as_bf16(w_out_hbm.at[idx], w_out_ref, stage_out_ref, sem_out)

    @pl.when(first)
    def _():
        hext_ref[0:CONV_HALO, :] = jnp.zeros((CONV_HALO, d), jnp.float32)
        hext_ref[n_ext:n_ext + V7X_SUBLANES, :] = jnp.zeros((V7X_SUBLANES, d), jnp.float32)

    @pl.when(jnp.logical_not(first))
    def _():
        hext_ref[0:CONV_HALO, :] = hext_ref[rows:n_ext, :]

    x = x_ref[0]
    xn = _rms(x, g_pre_ref[...]).astype(jnp.bfloat16)
    y = _bdot(xn, w_in_ref[...]) + b_in_ref[...]
    hext_ref[CONV_HALO:n_ext, :] = y[:, :d] * _sigmoid(y[:, d:])

    sub = lax.broadcasted_iota(jnp.int32, (V7X_SUBLANES, d), 0)
    n_tiles = n_ext // V7X_SUBLANES
    for s in range(1, V7X_SUBLANES):
        rot = [pltpu.roll(hext_ref[i * V7X_SUBLANES:(i + 1) * V7X_SUBLANES, :],
                          V7X_SUBLANES - s, axis=0) for i in range(n_tiles + 1)]
        for i in range(n_tiles):
            shift_ref[s - 1, i * V7X_SUBLANES:(i + 1) * V7X_SUBLANES, :] = jnp.where(
                sub < V7X_SUBLANES - s, rot[i], rot[i + 1])

    for j in range(CONV_WIDTH):
        taps_ref[j] = jnp.broadcast_to(w_dw_ref[j:j + 1, :], (V7X_SUBLANES, d))

    for c0 in range(0, d, CONV_LANES):
        for r0 in range(0, rows, CONV_SUB):
            acc = [jnp.zeros((V7X_SUBLANES, CONV_LANES), jnp.float32)
                   for _ in range(CONV_SUB // V7X_SUBLANES)]
            for j in range(CONV_WIDTH):
                off = CONV_HALO - (CONV_WIDTH - 1) + j
                lo = r0 + off - off % V7X_SUBLANES
                tap = taps_ref[j, :, c0:c0 + CONV_LANES]
                for i in range(len(acc)):
                    r = lo + i * V7X_SUBLANES
                    if off % V7X_SUBLANES == 0:
                        win = hext_ref[r:r + V7X_SUBLANES, c0:c0 + CONV_LANES]
                    else:
                        win = shift_ref[off % V7X_SUBLANES - 1, r:r + V7X_SUBLANES,
                                        c0:c0 + CONV_LANES]
                    acc[i] = acc[i] + tap * win
            for i in range(len(acc)):
                r = r0 + i * V7X_SUBLANES
                conv_ref[r:r + V7X_SUBLANES, c0:c0 + CONV_LANES] = acc[i]

    h = conv_ref[...] + b_dw_ref[...]
    mu = jnp.mean(h, axis=-1, keepdims=True)
    hc = h - mu
    hn = hc * lax.rsqrt(jnp.mean(hc * hc, axis=-1, keepdims=True) + EPS)
    hn = hn * ln_g_ref[...] + ln_b_ref[...]
    act = (hn * _sigmoid(hn)).astype(jnp.bfloat16)
    m = _bdot(act, w_out_ref[...]) + b_out_ref[...]
    o_ref[0] = x + _rms(m, g_post_ref[...])


def _conv_mixer(x, g_pre, g_post, w_in_all, b_in, w_dw, b_dw, ln_g, ln_b, w_out_all, b_out, *, idx):
    bsz, seq, d = x.shape
    assert seq % CONV_ROWS == 0 and CONV_ROWS % CONV_SUB == 0
    assert CONV_HALO >= CONV_WIDTH - 1 and CONV_HALO % V7X_SUBLANES == 0
    n_ext = CONV_HALO + CONV_ROWS
    row = pl.BlockSpec((1, CONV_ROWS, d), lambda b, s: (b, s, 0))
    vec = _resident((1, d))
    hbm = pl.BlockSpec(memory_space=pl.ANY)
    return pl.pallas_call(
        functools.partial(_conv_kernel, idx=idx),
        grid=(bsz, seq // CONV_ROWS),
        in_specs=[row, vec, vec, hbm, _resident((1, 2 * d)),
                  _resident(w_dw.shape), vec, vec, vec, hbm, vec],
        out_specs=row,
        out_shape=jax.ShapeDtypeStruct(x.shape, jnp.float32),
        scratch_shapes=[pltpu.VMEM((n_ext + V7X_SUBLANES, d), jnp.float32),
                        pltpu.VMEM((V7X_SUBLANES - 1, n_ext, d), jnp.float32),
                        pltpu.VMEM((CONV_WIDTH, V7X_SUBLANES, d), jnp.float32),
                        pltpu.VMEM((CONV_ROWS, d), jnp.float32)]
        + _weight_scratch(w_in_all.shape[1:]) + _weight_scratch(w_out_all.shape[1:]),
        compiler_params=pltpu.CompilerParams(
            dimension_semantics=("arbitrary", "arbitrary"), vmem_limit_bytes=VMEM_LIMIT_BYTES),
        name="conv_mixer",
    )(x, g_pre, g_post, w_in_all, b_in, w_dw, b_dw, ln_g, ln_b, w_out_all, b_out)


HGRN_LEVELS = tuple(HGRN_CHUNK >> (i + 1) for i in range(HGRN_CHUNK.bit_length() - 1))
HGRN_MXU_LEVELS = tuple(h for h in HGRN_LEVELS if 2 * h < V7X_SUBLANES)
LOG2_E = 1.4426950408889634


def _hgrn_sum_matrix():
    c = HGRN_CHUNK
    t = np.arange(c)[:, None]
    u = np.arange(c)[None, :]
    blocks = [u <= t]
    for h in HGRN_MXU_LEVELS:
        mid = (t // (2 * h)) * (2 * h) + h
        upper = t >= mid
        blocks.append(np.where(upper, (u >= mid) & (u <= t), (u > t) & (u < mid)))
    return np.concatenate(blocks, axis=0).astype(np.float32)


def _split3(x):
    hi = x.astype(jnp.bfloat16)
    r1 = x - hi.astype(jnp.float32)
    mid = r1.astype(jnp.bfloat16)
    lo = (r1 - mid.astype(jnp.float32)).astype(jnp.bfloat16)
    return hi, mid, lo


def _dot_nt(a, b):
    return lax.dot_general(a, b, (((1,), (1,)), ((), ())), preferred_element_type=jnp.float32)


def _dot_tn(a, b):
    return lax.dot_general(a, b, (((0,), (0,)), ((), ())), preferred_element_type=jnp.float32)


def _hgrn_kernel(x_ref, g_pre_ref, g_post_ref, w_in_hbm, lb_logits_ref, g_norm_ref, w_out_hbm,
                 sum_ref, o_ref, state_ref,
                 w_in_ref, stage_in_ref, sem_in, w_out_ref, stage_out_ref, sem_out, *, layer, idx):
    d = x_ref.shape[-1]
    rows = x_ref.shape[1]
    c = HGRN_CHUNK
    sub = HGRN_SUB
    n_sub = rows // sub
    chunks_per_sub = sub // c
    grp = HGRN_GROUP * c
    heads = [slice(hd * HEAD_DIM, (hd + 1) * HEAD_DIM) for hd in range(HGRN_HEADS)]

    @pl.when(jnp.logical_and(pl.program_id(0) == 0, pl.program_id(1) == 0))
    def _():
        _load_as_bf16(w_in_hbm.at[idx], w_in_ref, stage_in_ref, sem_in)
        _load_as_bf16(w_out_hbm.at[idx], w_out_ref, stage_out_ref, sem_out)

    @pl.when(pl.program_id(1) == 0)
    def _():
        state_ref[...] = jnp.zeros(state_ref.shape, jnp.float32)

    logits = lb_logits_ref[...]
    e = jnp.exp(logits - jnp.max(logits, axis=0, keepdims=True))
    p = e / jnp.sum(e, axis=0, keepdims=True)
    cum = p[0:1]
    for l in range(1, layer + 1):
        cum = cum + p[l:l + 1]
    lb = cum - p[0:1]

    tt = lax.broadcasted_iota(jnp.int32, (grp, grp), 0)
    ss = lax.broadcasted_iota(jnp.int32, (grp, grp), 1)
    diag_mask = tt == ss
    masks = []
    for h in HGRN_LEVELS:
        shift = (2 * h).bit_length() - 1
        masks.append(((tt >> shift) == (ss >> shift)) & ((tt & h) != 0) & ((ss & h) == 0))

    def project(si):
        x = x_ref[0, si * sub:(si + 1) * sub, :]
        xn = _rms(x, g_pre_ref[...]).astype(jnp.bfloat16)
        y = _bdot(xn, w_in_ref[...])
        q = y[:, 0:d]
        f = y[:, d:2 * d]
        q = q * _sigmoid(q)
        t = jnp.exp(-jnp.abs(f))
        w = 1.0 + t
        r = 1.0 / w
        a1 = jnp.log(lb)
        a2 = jnp.log1p(-lb) + (jnp.minimum(f, 0.0) - jnp.log(w))
        log_f = jnp.maximum(a1, a2) + jnp.log(1.0 + jnp.exp(-jnp.abs(a1 - a2)))
        k = (1.0 - lb) * jnp.where(f >= 0.0, t * r, r)
        v = y[:, 2 * d:3 * d].astype(jnp.bfloat16)
        return dict(x=x, q=q, k=k, v=v, lf2=log_f * LOG2_E, gate=y[:, 3 * d:4 * d])

    def exponents(t, ci):
        hi, mid, lo = _split3(t["lf2"][ci * c:(ci + 1) * c])
        return _bdot(sum_ref[...], jnp.concatenate([hi, mid, lo], axis=0))

    def chunk_factors(t, ci, ex):
        qc = t["q"][ci * c:(ci + 1) * c]
        kc = t["k"][ci * c:(ci + 1) * c]
        b = ex[0:c]
        b_last = b[c - 1:c]
        q_dec = (qc * jnp.exp2(b)).astype(jnp.bfloat16)
        k_dec = (kc * jnp.exp2(b_last - b)).astype(jnp.bfloat16)
        q_lv, k_lv = [], []
        mxu_block = 1
        for h in HGRN_LEVELS:
            if h in HGRN_MXU_LEVELS:
                expo = ex[mxu_block * c:(mxu_block + 1) * c]
                mxu_block += 1
            else:
                ref = jnp.concatenate(
                    [jnp.broadcast_to(b[g + h - 1:g + h], (2 * h, d)) for g in range(0, c, 2 * h)],
                    axis=0)
                expo = -jnp.abs(b - ref)
            fac = jnp.exp2(expo)
            q_lv.append((qc * fac).astype(jnp.bfloat16))
            k_lv.append((kc * fac).astype(jnp.bfloat16))
        return dict(q_lv=q_lv, k_lv=k_lv, qk=qc * kc, q_dec=q_dec, k_dec=k_dec,
                    decay_last=jnp.exp2(b_last))

    def group_scores(chunks):
        rows_of = lambda key, l=None: jnp.concatenate(
            [ch[key] if l is None else ch[key][l] for ch in chunks], axis=0)
        qk = rows_of("qk")
        q_lv = [rows_of("q_lv", l) for l in range(len(HGRN_LEVELS))]
        k_lv = [rows_of("k_lv", l) for l in range(len(HGRN_LEVELS))]
        scores = []
        for sl in heads:
            acc = jnp.where(diag_mask, jnp.sum(qk[:, sl], axis=-1, keepdims=True), 0.0)
            for ql, kl, mask in zip(q_lv, k_lv, masks):
                acc = acc + jnp.where(mask, _dot_nt(ql[:, sl], kl[:, sl]), 0.0)
            scores.append(acc.astype(jnp.bfloat16))
        return scores

    def state_path(t, ci, ch):
        vc = t["v"][ci * c:(ci + 1) * c]
        outs = []
        for hd, sl in enumerate(heads):
            state_t = state_ref[hd]
            outs.append(_dot_nt(ch["q_dec"][:, sl], state_t.astype(jnp.bfloat16)))
            state_ref[hd] = (state_t * ch["decay_last"][:, sl]
                             + _dot_tn(vc[:, sl], ch["k_dec"][:, sl]))
        return outs

    def finish(si, t, o):
        gate = t["gate"] * _sigmoid(t["gate"])
        parts = []
        for sl in heads:
            oh = o[:, sl]
            parts.append(oh * lax.rsqrt(jnp.mean(oh * oh, axis=-1, keepdims=True) + EPS))
        on = jnp.concatenate(parts, axis=-1) * g_norm_ref[...] * gate
        m = _bdot(on.astype(jnp.bfloat16), w_out_ref[...])
        o_ref[0, si * sub:(si + 1) * sub, :] = t["x"] + _rms(m, g_post_ref[...])

    tiles = [project(si) for si in range(n_sub)]
    pending = None
    for si, t in enumerate(tiles):
        exs = [exponents(t, ci) for ci in range(chunks_per_sub)]
        if pending is not None:
            finish(*pending)
        chunks = [chunk_factors(t, ci, exs[ci]) for ci in range(chunks_per_sub)]
        scores = [group_scores(chunks[g:g + HGRN_GROUP])
                  for g in range(0, chunks_per_sub, HGRN_GROUP)]
        o_inter = [state_path(t, ci, ch) for ci, ch in enumerate(chunks)]
        o_rows = []
        for gi, sc in enumerate(scores):
            vg = t["v"][gi * grp:(gi + 1) * grp]
            inter = [jnp.concatenate([o_inter[gi * HGRN_GROUP + j][hd] for j in range(HGRN_GROUP)],
                                     axis=0) for hd in range(HGRN_HEADS)]
            o_rows.append(jnp.concatenate(
                [inter[hd] + _bdot(sc[hd], vg[:, sl]) for hd, sl in enumerate(heads)], axis=-1))
        pending = (si, t, jnp.concatenate(o_rows, axis=0))
    finish(*pending)


def _hgrn_mixer(x, g_pre, g_post, w_in_all, lb_logits, g_norm, w_out_all, *, layer, idx):
    bsz, seq, d = x.shape
    assert seq % HGRN_ROWS == 0 and HGRN_ROWS % HGRN_SUB == 0
    assert HGRN_SUB % (HGRN_GROUP * HGRN_CHUNK) == 0
    assert d == HGRN_HEADS * HEAD_DIM and w_in_all.shape[1:] == (d, 4 * d)
    sum_mat = _hgrn_sum_matrix()
    sum3 = jnp.asarray(np.concatenate([sum_mat] * 3, axis=1), jnp.bfloat16)
    g_norm_row = jnp.tile(g_norm.reshape(1, HEAD_DIM), (1, HGRN_HEADS))
    row = pl.BlockSpec((1, HGRN_ROWS, d), lambda b, s: (b, s, 0))
    vec = _resident((1, d))
    hbm = pl.BlockSpec(memory_space=pl.ANY)
    return pl.pallas_call(
        functools.partial(_hgrn_kernel, layer=layer, idx=idx),
        grid=(bsz, seq // HGRN_ROWS),
        in_specs=[row, vec, vec, hbm, _resident(lb_logits.shape), vec, hbm,
                  _resident(sum3.shape)],
        out_specs=row,
        out_shape=jax.ShapeDtypeStruct(x.shape, jnp.float32),
        scratch_shapes=[pltpu.VMEM((HGRN_HEADS, HEAD_DIM, HEAD_DIM), jnp.float32)]
        + _weight_scratch(w_in_all.shape[1:]) + _weight_scratch(w_out_all.shape[1:]),
        compiler_params=pltpu.CompilerParams(
            dimension_semantics=("arbitrary", "arbitrary"), vmem_limit_bytes=VMEM_LIMIT_BYTES),
        name="hgrn_mixer",
    )(x, g_pre, g_post, w_in_all, lb_logits, g_norm_row, w_out_all, sum3)


def kernel(x, norm_gains, ffn_w_in, ffn_w_out, conv_w_in, conv_b_in, conv_w_dw, conv_b_dw,
           conv_ln_g, conv_ln_b, conv_w_out, conv_b_out, hgrn_w_in, hgrn_lb_logits,
           hgrn_g_norm, hgrn_w_out):
    bsz, seq, d = x.shape
    depth = norm_gains.shape[0]
    vec = lambda v: v.reshape(1, -1).astype(jnp.float32)

    def ffn(x, layer, which):
        g = norm_gains[layer]
        y = _ffn(x.reshape(bsz * seq, d), vec(g[4 * which]), vec(g[4 * which + 1]),
                 ffn_w_in, ffn_w_out, layer=layer, which=which)
        return y.reshape(bsz, seq, d)

    for layer in range(depth):
        g = norm_gains[layer]
        x = ffn(x, layer, 0)
        j = layer // 2
        if layer % 2 == 0:
            x = _conv_mixer(x, vec(g[2]), vec(g[3]), conv_w_in, vec(conv_b_in[j]),
                            conv_w_dw[j], vec(conv_b_dw[j]), vec(conv_ln_g[j]),
                            vec(conv_ln_b[j]), conv_w_out, vec(conv_b_out[j]), idx=j)
        else:
            x = _hgrn_mixer(x, vec(g[2]), vec(g[3]), hgrn_w_in, hgrn_lb_logits,
                            hgrn_g_norm[j], hgrn_w_out, layer=layer, idx=j)
        x = ffn(x, layer, 1)
    return x
```

```python
import functools

import jax
import jax.numpy as jnp
import numpy as np
from jax import lax
from jax.experimental import pallas as pl
from jax.experimental.pallas import tpu as pltpu

EPS = 1e-6
CONV_WIDTH = 31
HGRN_HEADS = 8
HEAD_DIM = 128

V7X_SUBLANES = 8
V7X_VMEM_BYTES = 64 * 1024 * 1024
VMEM_LIMIT_BYTES = 56 * 1024 * 1024

FFN_ROWS = 1024
FFN_COLS = 256
FFN_OUT_ROWS = 512
WEIGHT_STAGE_SLOTS = 4
WEIGHT_STAGE_BYTES = 1024 * 1024
BF16_ROWS_PER_VREG = 2 * V7X_SUBLANES
CONV_ROWS = 512
CONV_HALO = 32
CONV_SUB = 32
CONV_LANES = 512
HGRN_ROWS = 512
HGRN_SUB = 512
HGRN_CHUNK = 64
HGRN_GROUP = 2


def _rms(x, gain):
    return x * lax.rsqrt(jnp.mean(x * x, axis=-1, keepdims=True) + EPS) * gain


def _sigmoid(x):
    return 1.0 / (1.0 + jnp.exp(-x))


def _bdot(a, b):
    return jnp.dot(a, b, preferred_element_type=jnp.float32)


def _resident(shape):
    return pl.BlockSpec(shape, lambda *_: (0,) * len(shape), pipeline_mode=pl.Buffered(1))


def _weight_scratch(shape):
    k, n = shape
    cap = WEIGHT_STAGE_BYTES // (4 * n)
    stage_rows = max(r for r in range(BF16_ROWS_PER_VREG, cap + 1, BF16_ROWS_PER_VREG)
                     if k % r == 0)
    return [pltpu.VMEM((k, n), jnp.bfloat16),
            pltpu.VMEM((WEIGHT_STAGE_SLOTS, stage_rows, n), jnp.float32),
            pltpu.SemaphoreType.DMA((WEIGHT_STAGE_SLOTS,))]


def _load_as_bf16(src_hbm, dst_ref, stage_ref, sem):
    chunk = stage_ref.shape[1]
    n = src_hbm.shape[0] // chunk
    assert n * chunk == src_hbm.shape[0] and stage_ref.shape[0] == WEIGHT_STAGE_SLOTS

    def copy(c):
        slot = c % WEIGHT_STAGE_SLOTS
        return pltpu.make_async_copy(src_hbm.at[pl.ds(c * chunk, chunk), :],
                                     stage_ref.at[slot], sem.at[slot])

    ahead = WEIGHT_STAGE_SLOTS - 1
    for c in range(min(ahead, n)):
        copy(c).start()
    for c in range(n):
        if c + ahead < n:
            copy(c + ahead).start()
        copy(c).wait()
        dst_ref[c * chunk:(c + 1) * chunk, :] = stage_ref[c % WEIGHT_STAGE_SLOTS].astype(
            jnp.bfloat16)


def _ffn_kernel(x_ref, g_in_ref, g_out_ref, w_in_hbm, w_out_hbm, o_ref, act_ref,
                w_in_ref, stage_in_ref, sem_in, w_out_ref, stage_out_ref, sem_out,
                *, d_ff, layer, which):
    @pl.when(pl.program_id(0) == 0)
    def _():
        _load_as_bf16(w_in_hbm.at[layer, which], w_in_ref, stage_in_ref, sem_in)
        _load_as_bf16(w_out_hbm.at[layer, which], w_out_ref, stage_out_ref, sem_out)

    x = x_ref[...]
    xn = _rms(x, g_in_ref[...]).astype(jnp.bfloat16)
    for c0 in range(0, d_ff, FFN_COLS):
        gate = _bdot(xn, w_in_ref[:, c0:c0 + FFN_COLS])
        up = _bdot(xn, w_in_ref[:, d_ff + c0:d_ff + c0 + FFN_COLS])
        act_ref[:, c0:c0 + FFN_COLS] = (gate * _sigmoid(gate) * up).astype(jnp.bfloat16)
    for r0 in range(0, x.shape[0], FFN_OUT_ROWS):
        rs = slice(r0, r0 + FFN_OUT_ROWS)
        h = _bdot(act_ref[rs, :], w_out_ref[...])
        o_ref[rs, :] = x[rs] + 0.5 * _rms(h, g_out_ref[...])


def _ffn(x2d, g_in, g_out, w_in_all, w_out_all, *, layer, which):
    m, d = x2d.shape
    d_ff = w_out_all.shape[2]
    assert m % FFN_ROWS == 0 and d_ff % FFN_COLS == 0
    row = pl.BlockSpec((FFN_ROWS, d), lambda i: (i, 0))
    hbm = pl.BlockSpec(memory_space=pl.ANY)
    return pl.pallas_call(
        functools.partial(_ffn_kernel, d_ff=d_ff, layer=layer, which=which),
        grid=(m // FFN_ROWS,),
        in_specs=[row, _resident((1, d)), _resident((1, d)), hbm, hbm],
        out_specs=row,
        out_shape=jax.ShapeDtypeStruct((m, d), jnp.float32),
        scratch_shapes=[pltpu.VMEM((FFN_ROWS, d_ff), jnp.bfloat16)]
        + _weight_scratch(w_in_all.shape[2:]) + _weight_scratch(w_out_all.shape[2:]),
        compiler_params=pltpu.CompilerParams(
            dimension_semantics=("arbitrary",), vmem_limit_bytes=VMEM_LIMIT_BYTES),
        name="ffn",
    )(x2d, g_in, g_out, w_in_all, w_out_all)


def _conv_kernel(x_ref, g_pre_ref, g_post_ref, w_in_hbm, b_in_ref, w_dw_ref, b_dw_ref,
                 ln_g_ref, ln_b_ref, w_out_hbm, b_out_ref, o_ref,
                 hext_ref, shift_ref, taps_ref, conv_ref,
                 w_in_ref, stage_in_ref, sem_in, w_out_ref, stage_out_ref, sem_out, *, idx):
    d = x_ref.shape[-1]
    rows = x_ref.shape[1]
    n_ext = CONV_HALO + rows
    first = pl.program_id(1) == 0

    @pl.when(jnp.logical_and(pl.program_id(0) == 0, first))
    def _():
        _load_as_bf16(w_in_hbm.at[idx], w_in_ref, stage_in_ref, sem_in)
        _load_as_bf16(w_out_hbm.at[idx], w_out_ref, stage_out_ref, sem_out)

    @pl.when(first)
    def _():
        hext_ref[0:CONV_HALO, :] = jnp.zeros((CONV_HALO, d), jnp.float32)
        hext_ref[n_ext:n_ext + V7X_SUBLANES, :] = jnp.zeros((V7X_SUBLANES, d), jnp.float32)

    @pl.when(jnp.logical_not(first))
    def _():
        hext_ref[0:CONV_HALO, :] = hext_ref[rows:n_ext, :]

    x = x_ref[0]
    xn = _rms(x, g_pre_ref[...]).astype(jnp.bfloat16)
    y = _bdot(xn, w_in_ref[...]) + b_in_ref[...]
    hext_ref[CONV_HALO:n_ext, :] = y[:, :d] * _sigmoid(y[:, d:])

    sub = lax.broadcasted_iota(jnp.int32, (V7X_SUBLANES, d), 0)
    n_tiles = n_ext // V7X_SUBLANES
    for s in range(1, V7X_SUBLANES):
        rot = [pltpu.roll(hext_ref[i * V7X_SUBLANES:(i + 1) * V7X_SUBLANES, :],
                          V7X_SUBLANES - s, axis=0) for i in range(n_tiles + 1)]
        for i in range(n_tiles):
            shift_ref[s - 1, i * V7X_SUBLANES:(i + 1) * V7X_SUBLANES, :] = jnp.where(
                sub < V7X_SUBLANES - s, rot[i], rot[i + 1])

    for j in range(CONV_WIDTH):
        taps_ref[j] = jnp.broadcast_to(w_dw_ref[j:j + 1, :], (V7X_SUBLANES, d))

    for c0 in range(0, d, CONV_LANES):
        for r0 in range(0, rows, CONV_SUB):
            acc = [jnp.zeros((V7X_SUBLANES, CONV_LANES), jnp.float32)
                   for _ in range(CONV_SUB // V7X_SUBLANES)]
            for j in range(CONV_WIDTH):
                off = CONV_HALO - (CONV_WIDTH - 1) + j
                lo = r0 + off - off % V7X_SUBLANES
                tap = taps_ref[j, :, c0:c0 + CONV_LANES]
                for i in range(len(acc)):
                    r = lo + i * V7X_SUBLANES
                    if off % V7X_SUBLANES == 0:
                        win = hext_ref[r:r + V7X_SUBLANES, c0:c0 + CONV_LANES]
                    else:
                        win = shift_ref[off % V7X_SUBLANES - 1, r:r + V7X_SUBLANES,
                                        c0:c0 + CONV_LANES]
                    acc[i] = acc[i] + tap * win
            for i in range(len(acc)):
                r = r0 + i * V7X_SUBLANES
                conv_ref[r:r + V7X_SUBLANES, c0:c0 + CONV_LANES] = acc[i]

    h = conv_ref[...] + b_dw_ref[...]
    mu = jnp.mean(h, axis=-1, keepdims=True)
    hc = h - mu
    hn = hc * lax.rsqrt(jnp.mean(hc * hc, axis=-1, keepdims=True) + EPS)
    hn = hn * ln_g_ref[...] + ln_b_ref[...]
    act = (hn * _sigmoid(hn)).astype(jnp.bfloat16)
    m = _bdot(act, w_out_ref[...]) + b_out_ref[...]
    o_ref[0] = x + _rms(m, g_post_ref[...])


def _conv_mixer(x, g_pre, g_post, w_in_all, b_in, w_dw, b_dw, ln_g, ln_b, w_out_all, b_out, *, idx):
    bsz, seq, d = x.shape
    assert seq % CONV_ROWS == 0 and CONV_ROWS % CONV_SUB == 0
    assert CONV_HALO >= CONV_WIDTH - 1 and CONV_HALO % V7X_SUBLANES == 0
    n_ext = CONV_HALO + CONV_ROWS
    row = pl.BlockSpec((1, CONV_ROWS, d), lambda b, s: (b, s, 0))
    vec = _resident((1, d))
    hbm = pl.BlockSpec(memory_space=pl.ANY)
    return pl.pallas_call(
        functools.partial(_conv_kernel, idx=idx),
        grid=(bsz, seq // CONV_ROWS),
        in_specs=[row, vec, vec, hbm, _resident((1, 2 * d)),
                  _resident(w_dw.shape), vec, vec, vec, hbm, vec],
        out_specs=row,
        out_shape=jax.ShapeDtypeStruct(x.shape, jnp.float32),
        scratch_shapes=[pltpu.VMEM((n_ext + V7X_SUBLANES, d), jnp.float32),
                        pltpu.VMEM((V7X_SUBLANES - 1, n_ext, d), jnp.float32),
                        pltpu.VMEM((CONV_WIDTH, V7X_SUBLANES, d), jnp.float32),
                        pltpu.VMEM((CONV_ROWS, d), jnp.float32)]
        + _weight_scratch(w_in_all.shape[1:]) + _weight_scratch(w_out_all.shape[1:]),
        compiler_params=pltpu.CompilerParams(
            dimension_semantics=("arbitrary", "arbitrary"), vmem_limit_bytes=VMEM_LIMIT_BYTES),
        name="conv_mixer",
    )(x, g_pre, g_post, w_in_all, b_in, w_dw, b_dw, ln_g, ln_b, w_out_all, b_out)


HGRN_LEVELS = tuple(HGRN_CHUNK >> (i + 1) for i in range(HGRN_CHUNK.bit_length() - 1))
HGRN_MXU_LEVELS = tuple(h for h in HGRN_LEVELS if 2 * h < V7X_SUBLANES)
LOG2_E = 1.4426950408889634


def _hgrn_sum_matrix():
    c = HGRN_CHUNK
    t = np.arange(c)[:, None]
    u = np.arange(c)[None, :]
    blocks = [u <= t]
    for h in HGRN_MXU_LEVELS:
        mid = (t // (2 * h)) * (2 * h) + h
        upper = t >= mid
        blocks.append(np.where(upper, (u >= mid) & (u <= t), (u > t) & (u < mid)))
    return np.concatenate(blocks, axis=0).astype(np.float32)


def _split3(x):
    hi = x.astype(jnp.bfloat16)
    r1 = x - hi.astype(jnp.float32)
    mid = r1.astype(jnp.bfloat16)
    lo = (r1 - mid.astype(jnp.float32)).astype(jnp.bfloat16)
    return hi, mid, lo


def _dot_nt(a, b):
    return lax.dot_general(a, b, (((1,), (1,)), ((), ())), preferred_element_type=jnp.float32)


def _dot_tn(a, b):
    return lax.dot_general(a, b, (((0,), (0,)), ((), ())), preferred_element_type=jnp.float32)


def _hgrn_kernel(x_ref, g_pre_ref, g_post_ref, w_in_hbm, lb_logits_ref, g_norm_ref, w_out_hbm,
                 sum_ref, o_ref, state_ref,
                 w_in_ref, stage_in_ref, sem_in, w_out_ref, stage_out_ref, sem_out, *, layer, idx):
    d = x_ref.shape[-1]
    rows = x_ref.shape[1]
    c = HGRN_CHUNK
    sub = HGRN_SUB
    n_sub = rows // sub
    chunks_per_sub = sub // c
    grp = HGRN_GROUP * c
    heads = [slice(hd * HEAD_DIM, (hd + 1) * HEAD_DIM) for hd in range(HGRN_HEADS)]

    @pl.when(jnp.logical_and(pl.program_id(0) == 0, pl.program_id(1) == 0))
    def _():
        _load_as_bf16(w_in_hbm.at[idx], w_in_ref, stage_in_ref, sem_in)
        _load_as_bf16(w_out_hbm.at[idx], w_out_ref, stage_out_ref, sem_out)

    @pl.when(pl.program_id(1) == 0)
    def _():
        state_ref[...] = jnp.zeros(state_ref.shape, jnp.float32)

    logits = lb_logits_ref[...]
    e = jnp.exp(logits - jnp.max(logits, axis=0, keepdims=True))
    p = e / jnp.sum(e, axis=0, keepdims=True)
    cum = p[0:1]
    for l in range(1, layer + 1):
        cum = cum + p[l:l + 1]
    lb = cum - p[0:1]

    tt = lax.broadcasted_iota(jnp.int32, (grp, grp), 0)
    ss = lax.broadcasted_iota(jnp.int32, (grp, grp), 1)
    diag_mask = tt == ss
    masks = []
    for h in HGRN_LEVELS:
        shift = (2 * h).bit_length() - 1
        masks.append(((tt >> shift) == (ss >> shift)) & ((tt & h) != 0) & ((ss & h) == 0))

    def project(si):
        x = x_ref[0, si * sub:(si + 1) * sub, :]
        xn = _rms(x, g_pre_ref[...]).astype(jnp.bfloat16)
        y = _bdot(xn, w_in_ref[...])
        q = y[:, 0:d]
        f = y[:, d:2 * d]
        q = q * _sigmoid(q)
        t = jnp.exp(-jnp.abs(f))
        w = 1.0 + t
        r = 1.0 / w
        a1 = jnp.log(lb)
        a2 = jnp.log1p(-lb) + (jnp.minimum(f, 0.0) - jnp.log(w))
        log_f = jnp.maximum(a1, a2) + jnp.log(1.0 + jnp.exp(-jnp.abs(a1 - a2)))
        k = (1.0 - lb) * jnp.where(f >= 0.0, t * r, r)
        v = y[:, 2 * d:3 * d].astype(jnp.bfloat16)
        return dict(x=x, q=q, k=k, v=v, lf2=log_f * LOG2_E, gate=y[:, 3 * d:4 * d])

    def exponents(t, ci):
        hi, mid, lo = _split3(t["lf2"][ci * c:(ci + 1) * c])
        return _bdot(sum_ref[...], jnp.concatenate([hi, mid, lo], axis=0))

    def chunk_factors(t, ci, ex):
        qc = t["q"][ci * c:(ci + 1) * c]
        kc = t["k"][ci * c:(ci + 1) * c]
        b = ex[0:c]
        b_last = b[c - 1:c]
        q_dec = (qc * jnp.exp2(b)).astype(jnp.bfloat16)
        k_dec = (kc * jnp.exp2(b_last - b)).astype(jnp.bfloat16)
        q_lv, k_lv = [], []
        mxu_block = 1
        for h in HGRN_LEVELS:
            if h in HGRN_MXU_LEVELS:
                expo = ex[mxu_block * c:(mxu_block + 1) * c]
                mxu_block += 1
            else:
                ref = jnp.concatenate(
                    [jnp.broadcast_to(b[g + h - 1:g + h], (2 * h, d)) for g in range(0, c, 2 * h)],
                    axis=0)
                expo = -jnp.abs(b - ref)
            fac = jnp.exp2(expo)
            q_lv.append((qc * fac).astype(jnp.bfloat16))
            k_lv.append((kc * fac).astype(jnp.bfloat16))
        return dict(q_lv=q_lv, k_lv=k_lv, qk=qc * kc, q_dec=q_dec, k_dec=k_dec,
                    decay_last=jnp.exp2(b_last))

    def group_scores(chunks):
        rows_of = lambda key, l=None: jnp.concatenate(
            [ch[key] if l is None else ch[key][l] for ch in chunks], axis=0)
        qk = rows_of("qk")
        q_lv = [rows_of("q_lv", l) for l in range(len(HGRN_LEVELS))]
        k_lv = [rows_of("k_lv", l) for l in range(len(HGRN_LEVELS))]
        scores = []
        for sl in heads:
            acc = jnp.where(diag_mask, jnp.sum(qk[:, sl], axis=-1, keepdims=True), 0.0)
            for ql, kl, mask in zip(q_lv, k_lv, masks):
                acc = acc + jnp.where(mask, _dot_nt(ql[:, sl], kl[:, sl]), 0.0)
            scores.append(acc.astype(jnp.bfloat16))
        return scores

    def state_path(t, ci, ch):
        vc = t["v"][ci * c:(ci + 1) * c]
        outs = []
        for hd, sl in enumerate(heads):
            state_t = state_ref[hd]
            outs.append(_dot_nt(ch["q_dec"][:, sl], state_t.astype(jnp.bfloat16)))
            state_ref[hd] = (state_t * ch["decay_last"][:, sl]
                             + _dot_tn(vc[:, sl], ch["k_dec"][:, sl]))
        return outs

    def finish(si, t, o):
        gate = t["gate"] * _sigmoid(t["gate"])
        parts = []
        for sl in heads:
            oh = o[:, sl]
            parts.append(oh * lax.rsqrt(jnp.mean(oh * oh, axis=-1, keepdims=True) + EPS))
        on = jnp.concatenate(parts, axis=-1) * g_norm_ref[...] * gate
        m = _bdot(on.astype(jnp.bfloat16), w_out_ref[...])
        o_ref[0, si * sub:(si + 1) * sub, :] = t["x"] + _rms(m, g_post_ref[...])

    tiles = [project(si) for si in range(n_sub)]
    pending = None
    for si, t in enumerate(tiles):
        exs = [exponents(t, ci) for ci in range(chunks_per_sub)]
        if pending is not None:
            finish(*pending)
        chunks = [chunk_factors(t, ci, exs[ci]) for ci in range(chunks_per_sub)]
        scores = [group_scores(chunks[g:g + HGRN_GROUP])
                  for g in range(0, chunks_per_sub, HGRN_GROUP)]
        o_inter = [state_path(t, ci, ch) for ci, ch in enumerate(chunks)]
        o_rows = []
        for gi, sc in enumerate(scores):
            vg = t["v"][gi * grp:(gi + 1) * grp]
            inter = [jnp.concatenate([o_inter[gi * HGRN_GROUP + j][hd] for j in range(HGRN_GROUP)],
                                     axis=0) for hd in range(HGRN_HEADS)]
            o_rows.append(jnp.concatenate(
                [inter[hd] + _bdot(sc[hd], vg[:, sl]) for hd, sl in enumerate(heads)], axis=-1))
        pending = (si, t, jnp.concatenate(o_rows, axis=0))
    finish(*pending)


def _hgrn_mixer(x, g_pre, g_post, w_in_all, lb_logits, g_norm, w_out_all, *, layer, idx):
    bsz, seq, d = x.shape
    assert seq % HGRN_ROWS == 0 and HGRN_ROWS % HGRN_SUB == 0
    assert HGRN_SUB % (HGRN_GROUP * HGRN_CHUNK) == 0
    assert d == HGRN_HEADS * HEAD_DIM and w_in_all.shape[1:] == (d, 4 * d)
    sum_mat = _hgrn_sum_matrix()
    sum3 = jnp.asarray(np.concatenate([sum_mat] * 3, axis=1), jnp.bfloat16)
    g_norm_row = jnp.tile(g_norm.reshape(1, HEAD_DIM), (1, HGRN_HEADS))
    row = pl.BlockSpec((1, HGRN_ROWS, d), lambda b, s: (b, s, 0))
    vec = _resident((1, d))
    hbm = pl.BlockSpec(memory_space=pl.ANY)
    return pl.pallas_call(
        functools.partial(_hgrn_kernel, layer=layer, idx=idx),
        grid=(bsz, seq // HGRN_ROWS),
        in_specs=[row, vec, vec, hbm, _resident(lb_logits.shape), vec, hbm,
                  _resident(sum3.shape)],
        out_specs=row,
        out_shape=jax.ShapeDtypeStruct(x.shape, jnp.float32),
        scratch_shapes=[pltpu.VMEM((HGRN_HEADS, HEAD_DIM, HEAD_DIM), jnp.float32)]
        + _weight_scratch(w_in_all.shape[1:]) + _weight_scratch(w_out_all.shape[1:]),
        compiler_params=pltpu.CompilerParams(
            dimension_semantics=("arbitrary", "arbitrary"), vmem_limit_bytes=VMEM_LIMIT_BYTES),
        name="hgrn_mixer",
    )(x, g_pre, g_post, w_in_all, lb_logits, g_norm_row, w_out_all, sum3)


def kernel(x, norm_gains, ffn_w_in, ffn_w_out, conv_w_in, conv_b_in, conv_w_dw, conv_b_dw,
           conv_ln_g, conv_ln_b, conv_w_out, conv_b_out, hgrn_w_in, hgrn_lb_logits,
           hgrn_g_norm, hgrn_w_out):
    bsz, seq, d = x.shape
    depth = norm_gains.shape[0]
    vec = lambda v: v.reshape(1, -1).astype(jnp.float32)

    def ffn(x, layer, which):
        g = norm_gains[layer]
        y = _ffn(x.reshape(bsz * seq, d), vec(g[4 * which]), vec(g[4 * which + 1]),
                 ffn_w_in, ffn_w_out, layer=layer, which=which)
        return y.reshape(bsz, seq, d)

    for layer in range(depth):
        g = norm_gains[layer]
        x = ffn(x, layer, 0)
        j = layer // 2
        if layer % 2 == 0:
            x = _conv_mixer(x, vec(g[2]), vec(g[3]), conv_w_in, vec(conv_b_in[j]),
                            conv_w_dw[j], vec(conv_b_dw[j]), vec(conv_ln_g[j]),
                            vec(conv_ln_b[j]), conv_w_out, vec(conv_b_out[j]), idx=j)
        else:
            x = _hgrn_mixer(x, vec(g[2]), vec(g[3]), hgrn_w_in, hgrn_lb_logits,
                            hgrn_g_norm[j], hgrn_w_out, layer=layer, idx=j)
        x = ffn(x, layer, 1)
    return x
```

```python
import functools

import jax
import jax.numpy as jnp
import numpy as np
from jax import lax
from jax.experimental import pallas as pl
from jax.experimental.pallas import tpu as pltpu

EPS = 1e-6
CONV_WIDTH = 31
HGRN_HEADS = 8
HEAD_DIM = 128

V7X_SUBLANES = 8
V7X_VMEM_BYTES = 64 * 1024 * 1024
VMEM_LIMIT_BYTES = 56 * 1024 * 1024

FFN_ROWS = 1024
FFN_COLS = 256
FFN_OUT_ROWS = 512
WEIGHT_STAGE_SLOTS = 4
WEIGHT_STAGE_BYTES = 1024 * 1024
BF16_ROWS_PER_VREG = 2 * V7X_SUBLANES
CONV_ROWS = 512
CONV_HALO = 32
CONV_SUB = 32
CONV_LANES = 512
HGRN_ROWS = 512
HGRN_SUB = 512
HGRN_CHUNK = 64
HGRN_GROUP = 2


def _rms(x, gain):
    return x * lax.rsqrt(jnp.mean(x * x, axis=-1, keepdims=True) + EPS) * gain


def _sigmoid(x):
    return 1.0 / (1.0 + jnp.exp(-x))


def _bdot(a, b):
    return jnp.dot(a, b, preferred_element_type=jnp.float32)


def _resident(shape):
    return pl.BlockSpec(shape, lambda *_: (0,) * len(shape), pipeline_mode=pl.Buffered(1))


def _weight_scratch(shape):
    k, n = shape
    cap = WEIGHT_STAGE_BYTES // (4 * n)
    stage_rows = max(r for r in range(BF16_ROWS_PER_VREG, cap + 1, BF16_ROWS_PER_VREG)
                     if k % r == 0)
    return [pltpu.VMEM((k, n), jnp.bfloat16),
            pltpu.VMEM((WEIGHT_STAGE_SLOTS, stage_rows, n), jnp.float32),
            pltpu.SemaphoreType.DMA((WEIGHT_STAGE_SLOTS,))]


def _load_as_bf16(src_hbm, dst_ref, stage_ref, sem):
    chunk = stage_ref.shape[1]
    n = src_hbm.shape[0] // chunk
    assert n * chunk == src_hbm.shape[0] and stage_ref.shape[0] == WEIGHT_STAGE_SLOTS

    def copy(c):
        slot = c % WEIGHT_STAGE_SLOTS
        return pltpu.make_async_copy(src_hbm.at[pl.ds(c * chunk, chunk), :],
                                     stage_ref.at[slot], sem.at[slot])

    ahead = WEIGHT_STAGE_SLOTS - 1
    for c in range(min(ahead, n)):
        copy(c).start()
    for c in range(n):
        if c + ahead < n:
            copy(c + ahead).start()
        copy(c).wait()
        dst_ref[c * chunk:(c + 1) * chunk, :] = stage_ref[c % WEIGHT_STAGE_SLOTS].astype(
            jnp.bfloat16)


def _ffn_kernel(x_ref, g_in_ref, g_out_ref, w_in_hbm, w_out_hbm, o_ref, act_ref,
                w_in_ref, stage_in_ref, sem_in, w_out_ref, stage_out_ref, sem_out,
                *, d_ff, layer, which):
    @pl.when(pl.program_id(0) == 0)
    def _():
        _load_as_bf16(w_in_hbm.at[layer, which], w_in_ref, stage_in_ref, sem_in)
        _load_as_bf16(w_out_hbm.at[layer, which], w_out_ref, stage_out_ref, sem_out)

    x = x_ref[...]
    xn = _rms(x, g_in_ref[...]).astype(jnp.bfloat16)
    for c0 in range(0, d_ff, FFN_COLS):
        gate = _bdot(xn, w_in_ref[:, c0:c0 + FFN_COLS])
        up = _bdot(xn, w_in_ref[:, d_ff + c0:d_ff + c0 + FFN_COLS])
        act_ref[:, c0:c0 + FFN_COLS] = (gate * _sigmoid(gate) * up).astype(jnp.bfloat16)
    for r0 in range(0, x.shape[0], FFN_OUT_ROWS):
        rs = slice(r0, r0 + FFN_OUT_ROWS)
        h = _bdot(act_ref[rs, :], w_out_ref[...])
        o_ref[rs, :] = x[rs] + 0.5 * _rms(h, g_out_ref[...])


def _ffn(x2d, g_in, g_out, w_in_all, w_out_all, *, layer, which):
    m, d = x2d.shape
    d_ff = w_out_all.shape[2]
    assert m % FFN_ROWS == 0 and d_ff % FFN_COLS == 0
    row = pl.BlockSpec((FFN_ROWS, d), lambda i: (i, 0))
    hbm = pl.BlockSpec(memory_space=pl.ANY)
    return pl.pallas_call(
        functools.partial(_ffn_kernel, d_ff=d_ff, layer=layer, which=which),
        grid=(m // FFN_ROWS,),
        in_specs=[row, _resident((1, d)), _resident((1, d)), hbm, hbm],
        out_specs=row,
        out_shape=jax.ShapeDtypeStruct((m, d), jnp.float32),
        scratch_shapes=[pltpu.VMEM((FFN_ROWS, d_ff), jnp.bfloat16)]
        + _weight_scratch(w_in_all.shape[2:]) + _weight_scratch(w_out_all.shape[2:]),
        compiler_params=pltpu.CompilerParams(
            dimension_semantics=("arbitrary",), vmem_limit_bytes=VMEM_LIMIT_BYTES),
        name="ffn",
    )(x2d, g_in, g_out, w_in_all, w_out_all)


def _conv_kernel(x_ref, g_pre_ref, g_post_ref, w_in_hbm, b_in_ref, w_dw_ref, b_dw_ref,
                 ln_g_ref, ln_b_ref, w_out_hbm, b_out_ref, o_ref,
                 hext_ref, shift_ref, taps_ref, conv_ref,
                 w_in_ref, stage_in_ref, sem_in, w_out_ref, stage_out_ref, sem_out, *, idx):
    d = x_ref.shape[-1]
    rows = x_ref.shape[1]
    n_ext = CONV_HALO + rows
    first = pl.program_id(1) == 0

    @pl.when(jnp.logical_and(pl.program_id(0) == 0, first))
    def _():
        _load_as_bf16(w_in_hbm.at[idx], w_in_ref, stage_in_ref, sem_in)
        _load_as_bf16(w_out_hbm.at[idx], w_out_ref, stage_out_ref, sem_out)

    @pl.when(first)
    def _():
        hext_ref[0:CONV_HALO, :] = jnp.zeros((CONV_HALO, d), jnp.float32)
        hext_ref[n_ext:n_ext + V7X_SUBLANES, :] = jnp.zeros((V7X_SUBLANES, d), jnp.float32)

    @pl.when(jnp.logical_not(first))
    def _():
        hext_ref[0:CONV_HALO, :] = hext_ref[rows:n_ext, :]

    x = x_ref[0]
    xn = _rms(x, g_pre_ref[...]).astype(jnp.bfloat16)
    y = _bdot(xn, w_in_ref[...]) + b_in_ref[...]
    hext_ref[CONV_HALO:n_ext, :] = y[:, :d] * _sigmoid(y[:, d:])

    sub = lax.broadcasted_iota(jnp.int32, (V7X_SUBLANES, d), 0)
    n_tiles = n_ext // V7X_SUBLANES
    for s in range(1, V7X_SUBLANES):
        rot = [pltpu.roll(hext_ref[i * V7X_SUBLANES:(i + 1) * V7X_SUBLANES, :],
                          V7X_SUBLANES - s, axis=0) for i in range(n_tiles + 1)]
        for i in range(n_tiles):
            shift_ref[s - 1, i * V7X_SUBLANES:(i + 1) * V7X_SUBLANES, :] = jnp.where(
                sub < V7X_SUBLANES - s, rot[i], rot[i + 1])

    for j in range(CONV_WIDTH):
        taps_ref[j] = jnp.broadcast_to(w_dw_ref[j:j + 1, :], (V7X_SUBLANES, d))

    for c0 in range(0, d, CONV_LANES):
        for r0 in range(0, rows, CONV_SUB):
            acc = [jnp.zeros((V7X_SUBLANES, CONV_LANES), jnp.float32)
                   for _ in range(CONV_SUB // V7X_SUBLANES)]
            for j in range(CONV_WIDTH):
                off = CONV_HALO - (CONV_WIDTH - 1) + j
                lo = r0 + off - off % V7X_SUBLANES
                tap = taps_ref[j, :, c0:c0 + CONV_LANES]
                for i in range(len(acc)):
                    r = lo + i * V7X_SUBLANES
                    if off % V7X_SUBLANES == 0:
                        win = hext_ref[r:r + V7X_SUBLANES, c0:c0 + CONV_LANES]
                    else:
                        win = shift_ref[off % V7X_SUBLANES - 1, r:r + V7X_SUBLANES,
                                        c0:c0 + CONV_LANES]
                    acc[i] = acc[i] + tap * win
            for i in range(len(acc)):
                r = r0 + i * V7X_SUBLANES
                conv_ref[r:r + V7X_SUBLANES, c0:c0 + CONV_LANES] = acc[i]

    h = conv_ref[...] + b_dw_ref[...]
    mu = jnp.mean(h, axis=-1, keepdims=True)
    hc = h - mu
    hn = hc * lax.rsqrt(jnp.mean(hc * hc, axis=-1, keepdims=True) + EPS)
    hn = hn * ln_g_ref[...] + ln_b_ref[...]
    act = (hn * _sigmoid(hn)).astype(jnp.bfloat16)
    m = _bdot(act, w_out_ref[...]) + b_out_ref[...]
    o_ref[0] = x + _rms(m, g_post_ref[...])


def _conv_mixer(x, g_pre, g_post, w_in_all, b_in, w_dw, b_dw, ln_g, ln_b, w_out_all, b_out, *, idx):
    bsz, seq, d = x.shape
    assert seq % CONV_ROWS == 0 and CONV_ROWS % CONV_SUB == 0
    assert CONV_HALO >= CONV_WIDTH - 1 and CONV_HALO % V7X_SUBLANES == 0
    n_ext = CONV_HALO + CONV_ROWS
    row = pl.BlockSpec((1, CONV_ROWS, d), lambda b, s: (b, s, 0))
    vec = _resident((1, d))
    hbm = pl.BlockSpec(memory_space=pl.ANY)
    return pl.pallas_call(
        functools.partial(_conv_kernel, idx=idx),
        grid=(bsz, seq // CONV_ROWS),
        in_specs=[row, vec, vec, hbm, _resident((1, 2 * d)),
                  _resident(w_dw.shape), vec, vec, vec, hbm, vec],
        out_specs=row,
        out_shape=jax.ShapeDtypeStruct(x.shape, jnp.float32),
        scratch_shapes=[pltpu.VMEM((n_ext + V7X_SUBLANES, d), jnp.float32),
                        pltpu.VMEM((V7X_SUBLANES - 1, n_ext, d), jnp.float32),
                        pltpu.VMEM((CONV_WIDTH, V7X_SUBLANES, d), jnp.float32),
                        pltpu.VMEM((CONV_ROWS, d), jnp.float32)]
        + _weight_scratch(w_in_all.shape[1:]) + _weight_scratch(w_out_all.shape[1:]),
        compiler_params=pltpu.CompilerParams(
            dimension_semantics=("arbitrary", "arbitrary"), vmem_limit_bytes=VMEM_LIMIT_BYTES),
        name="conv_mixer",
    )(x, g_pre, g_post, w_in_all, b_in, w_dw, b_dw, ln_g, ln_b, w_out_all, b_out)


HGRN_LEVELS = tuple(HGRN_CHUNK >> (i + 1) for i in range(HGRN_CHUNK.bit_length() - 1))
HGRN_MXU_LEVELS = tuple(h for h in HGRN_LEVELS if 2 * h < V7X_SUBLANES)
LOG2_E = 1.4426950408889634


def _hgrn_sum_matrix():
    c = HGRN_CHUNK
    t = np.arange(c)[:, None]
    u = np.arange(c)[None, :]
    blocks = [u <= t]
    for h in HGRN_MXU_LEVELS:
        mid = (t // (2 * h)) * (2 * h) + h
        upper = t >= mid
        blocks.append(np.where(upper, (u >= mid) & (u <= t), (u > t) & (u < mid)))
    return np.concatenate(blocks, axis=0).astype(np.float32)


def _split3(x):
    hi = x.astype(jnp.bfloat16)
    r1 = x - hi.astype(jnp.float32)
    mid = r1.astype(jnp.bfloat16)
    lo = (r1 - mid.astype(jnp.float32)).astype(jnp.bfloat16)
    return hi, mid, lo


def _dot_nt(a, b):
    return lax.dot_general(a, b, (((1,), (1,)), ((), ())), preferred_element_type=jnp.float32)


def _dot_tn(a, b):
    return lax.dot_general(a, b, (((0,), (0,)), ((), ())), preferred_element_type=jnp.float32)


def _hgrn_kernel(x_ref, g_pre_ref, g_post_ref, w_in_hbm, lb_logits_ref, g_norm_ref, w_out_hbm,
                 sum_ref, o_ref, state_ref,
                 w_in_ref, stage_in_ref, sem_in, w_out_ref, stage_out_ref, sem_out, *, layer, idx):
    d = x_ref.shape[-1]
    rows = x_ref.shape[1]
    c = HGRN_CHUNK
    sub = HGRN_SUB
    n_sub = rows // sub
    chunks_per_sub = sub // c
    grp = HGRN_GROUP * c
    heads = [slice(hd * HEAD_DIM, (hd + 1) * HEAD_DIM) for hd in range(HGRN_HEADS)]

    @pl.when(jnp.logical_and(pl.program_id(0) == 0, pl.program_id(1) == 0))
    def _():
        _load_as_bf16(w_in_hbm.at[idx], w_in_ref, stage_in_ref, sem_in)
        _load_as_bf16(w_out_hbm.at[idx], w_out_ref, stage_out_ref, sem_out)

    @pl.when(pl.program_id(1) == 0)
    def _():
        state_ref[...] = jnp.zeros(state_ref.shape, jnp.float32)

    logits = lb_logits_ref[...]
    e = jnp.exp(logits - jnp.max(logits, axis=0, keepdims=True))
    p = e / jnp.sum(e, axis=0, keepdims=True)
    cum = p[0:1]
    for l in range(1, layer + 1):
        cum = cum + p[l:l + 1]
    lb = cum - p[0:1]

    tt = lax.broadcasted_iota(jnp.int32, (grp, grp), 0)
    ss = lax.broadcasted_iota(jnp.int32, (grp, grp), 1)
    diag_mask = tt == ss
    masks = []
    for h in HGRN_LEVELS:
        shift = (2 * h).bit_length() - 1
        masks.append(((tt >> shift) == (ss >> shift)) & ((tt & h) != 0) & ((ss & h) == 0))

    def project(si):
        x = x_ref[0, si * sub:(si + 1) * sub, :]
        xn = _rms(x, g_pre_ref[...]).astype(jnp.bfloat16)
        f = _bdot(xn, w_in_ref[:, d:2 * d])
        q = _bdot(xn, w_in_ref[:, 0:d])
        v = _bdot(xn, w_in_ref[:, 2 * d:3 * d]).astype(jnp.bfloat16)
        gate = _bdot(xn, w_in_ref[:, 3 * d:4 * d])
        q = q * _sigmoid(q)
        t = jnp.exp(-jnp.abs(f))
        w = 1.0 + t
        r = 1.0 / w
        a1 = jnp.log(lb)
        a2 = jnp.log1p(-lb) + (jnp.minimum(f, 0.0) - jnp.log(w))
        log_f = jnp.maximum(a1, a2) + jnp.log(1.0 + jnp.exp(-jnp.abs(a1 - a2)))
        k = (1.0 - lb) * jnp.where(f >= 0.0, t * r, r)
        return dict(x=x, q=q, k=k, v=v, lf2=log_f * LOG2_E, gate=gate)

    def exponents(t, ci):
        hi, mid, lo = _split3(t["lf2"][ci * c:(ci + 1) * c])
        return _bdot(sum_ref[...], jnp.concatenate([hi, mid, lo], axis=0))

    def chunk_factors(t, ci, ex):
        qc = t["q"][ci * c:(ci + 1) * c]
        kc = t["k"][ci * c:(ci + 1) * c]
        b = ex[0:c]
        b_last = b[c - 1:c]
        q_dec = (qc * jnp.exp2(b)).astype(jnp.bfloat16)
        k_dec = (kc * jnp.exp2(b_last - b)).astype(jnp.bfloat16)
        q_bf, k_bf = qc.astype(jnp.bfloat16), kc.astype(jnp.bfloat16)
        q_lv, k_lv = [], []
        mxu_block = 1
        for h in HGRN_LEVELS:
            if h in HGRN_MXU_LEVELS:
                expo = ex[mxu_block * c:(mxu_block + 1) * c]
                mxu_block += 1
            else:
                ref = jnp.concatenate(
                    [jnp.broadcast_to(b[g + h - 1:g + h], (2 * h, d)) for g in range(0, c, 2 * h)],
                    axis=0)
                expo = -jnp.abs(b - ref)
            fac = jnp.exp2(expo).astype(jnp.bfloat16)
            q_lv.append(q_bf * fac)
            k_lv.append(k_bf * fac)
        return dict(q_lv=q_lv, k_lv=k_lv, qk=qc * kc, q_dec=q_dec, k_dec=k_dec,
                    decay_last=jnp.exp2(b_last))

    def group_scores(chunks):
        rows_of = lambda key, l=None: jnp.concatenate(
            [ch[key] if l is None else ch[key][l] for ch in chunks], axis=0)
        qk = rows_of("qk")
        q_lv = [rows_of("q_lv", l) for l in range(len(HGRN_LEVELS))]
        k_lv = [rows_of("k_lv", l) for l in range(len(HGRN_LEVELS))]
        scores = []
        for sl in heads:
            acc = jnp.where(diag_mask, jnp.sum(qk[:, sl], axis=-1, keepdims=True), 0.0)
            for ql, kl, mask in zip(q_lv, k_lv, masks):
                acc = acc + jnp.where(mask, _dot_nt(ql[:, sl], kl[:, sl]), 0.0)
            scores.append(acc.astype(jnp.bfloat16))
        return scores

    def state_path(t, ci, ch):
        vc = t["v"][ci * c:(ci + 1) * c]
        outs = []
        for hd, sl in enumerate(heads):
            state_t = state_ref[hd]
            outs.append(_dot_nt(ch["q_dec"][:, sl], state_t.astype(jnp.bfloat16)))
            state_ref[hd] = (state_t * ch["decay_last"][:, sl]
                             + _dot_tn(vc[:, sl], ch["k_dec"][:, sl]))
        return outs

    def finish(si, t, o):
        gate = t["gate"] * _sigmoid(t["gate"])
        parts = []
        for sl in heads:
            oh = o[:, sl]
            parts.append(oh * lax.rsqrt(jnp.mean(oh * oh, axis=-1, keepdims=True) + EPS))
        on = jnp.concatenate(parts, axis=-1) * g_norm_ref[...] * gate
        m = _bdot(on.astype(jnp.bfloat16), w_out_ref[...])
        o_ref[0, si * sub:(si + 1) * sub, :] = t["x"] + _rms(m, g_post_ref[...])

    tiles = [project(si) for si in range(n_sub)]
    pending = None
    for si, t in enumerate(tiles):
        exs = [exponents(t, ci) for ci in range(chunks_per_sub)]
        if pending is not None:
            finish(*pending)
        chunks = [chunk_factors(t, ci, exs[ci]) for ci in range(chunks_per_sub)]
        scores = [group_scores(chunks[g:g + HGRN_GROUP])
                  for g in range(0, chunks_per_sub, HGRN_GROUP)]
        o_inter = [state_path(t, ci, ch) for ci, ch in enumerate(chunks)]
        o_rows = []
        for gi, sc in enumerate(scores):
            vg = t["v"][gi * grp:(gi + 1) * grp]
            inter = [jnp.concatenate([o_inter[gi * HGRN_GROUP + j][hd] for j in range(HGRN_GROUP)],
                                     axis=0) for hd in range(HGRN_HEADS)]
            o_rows.append(jnp.concatenate(
                [inter[hd] + _bdot(sc[hd], vg[:, sl]) for hd, sl in enumerate(heads)], axis=-1))
        pending = (si, t, jnp.concatenate(o_rows, axis=0))
    finish(*pending)


def _hgrn_mixer(x, g_pre, g_post, w_in_all, lb_logits, g_norm, w_out_all, *, layer, idx):
    bsz, seq, d = x.shape
    assert seq % HGRN_ROWS == 0 and HGRN_ROWS % HGRN_SUB == 0
    assert HGRN_SUB % (HGRN_GROUP * HGRN_CHUNK) == 0
    assert d == HGRN_HEADS * HEAD_DIM and w_in_all.shape[1:] == (d, 4 * d)
    sum_mat = _hgrn_sum_matrix()
    sum3 = jnp.asarray(np.concatenate([sum_mat] * 3, axis=1), jnp.bfloat16)
    g_norm_row = jnp.tile(g_norm.reshape(1, HEAD_DIM), (1, HGRN_HEADS))
    row = pl.BlockSpec((1, HGRN_ROWS, d), lambda b, s: (b, s, 0))
    vec = _resident((1, d))
    hbm = pl.BlockSpec(memory_space=pl.ANY)
    return pl.pallas_call(
        functools.partial(_hgrn_kernel, layer=layer, idx=idx),
        grid=(bsz, seq // HGRN_ROWS),
        in_specs=[row, vec, vec, hbm, _resident(lb_logits.shape), vec, hbm,
                  _resident(sum3.shape)],
        out_specs=row,
        out_shape=jax.ShapeDtypeStruct(x.shape, jnp.float32),
        scratch_shapes=[pltpu.VMEM((HGRN_HEADS, HEAD_DIM, HEAD_DIM), jnp.float32)]
        + _weight_scratch(w_in_all.shape[1:]) + _weight_scratch(w_out_all.shape[1:]),
        compiler_params=pltpu.CompilerParams(
            dimension_semantics=("arbitrary", "arbitrary"), vmem_limit_bytes=VMEM_LIMIT_BYTES),
        name="hgrn_mixer",
    )(x, g_pre, g_post, w_in_all, lb_logits, g_norm_row, w_out_all, sum3)


def kernel(x, norm_gains, ffn_w_in, ffn_w_out, conv_w_in, conv_b_in, conv_w_dw, conv_b_dw,
           conv_ln_g, conv_ln_b, conv_w_out, conv_b_out, hgrn_w_in, hgrn_lb_logits,
           hgrn_g_norm, hgrn_w_out):
    bsz, seq, d = x.shape
    depth = norm_gains.shape[0]
    vec = lambda v: v.reshape(1, -1).astype(jnp.float32)

    def ffn(x, layer, which):
        g = norm_gains[layer]
        y = _ffn(x.reshape(bsz * seq, d), vec(g[4 * which]), vec(g[4 * which + 1]),
                 ffn_w_in, ffn_w_out, layer=layer, which=which)
        return y.reshape(bsz, seq, d)

    for layer in range(depth):
        g = norm_gains[layer]
        x = ffn(x, layer, 0)
        j = layer // 2
        if layer % 2 == 0:
            x = _conv_mixer(x, vec(g[2]), vec(g[3]), conv_w_in, vec(conv_b_in[j]),
                            conv_w_dw[j], vec(conv_b_dw[j]), vec(conv_ln_g[j]),
                            vec(conv_ln_b[j]), conv_w_out, vec(conv_b_out[j]), idx=j)
        else:
            x = _hgrn_mixer(x, vec(g[2]), vec(g[3]), hgrn_w_in, hgrn_lb_logits,
                            hgrn_g_norm[j], hgrn_w_out, layer=layer, idx=j)
        x = ffn(x, layer, 1)
    return x
```

```python
import functools

import jax
import jax.numpy as jnp
import numpy as np
from jax import lax
from jax.experimental import pallas as pl
from jax.experimental.pallas import tpu as pltpu

EPS = 1e-6
CONV_WIDTH = 31
HGRN_HEADS = 8
HEAD_DIM = 128

V7X_SUBLANES = 8
V7X_VMEM_BYTES = 64 * 1024 * 1024
VMEM_LIMIT_BYTES = 56 * 1024 * 1024

FFN_ROWS = 1024
FFN_COLS = 256
FFN_OUT_ROWS = 512
WEIGHT_STAGE_SLOTS = 4
WEIGHT_STAGE_BYTES = 1024 * 1024
BF16_ROWS_PER_VREG = 2 * V7X_SUBLANES
CONV_ROWS = 512
CONV_HALO = 32
CONV_SUB = 32
CONV_LANES = 256
HGRN_ROWS = 512
HGRN_SUB = 512
HGRN_CHUNK = 64
HGRN_GROUP = 2


def _rms(x, gain):
    return x * lax.rsqrt(jnp.mean(x * x, axis=-1, keepdims=True) + EPS) * gain


def _sigmoid(x):
    return 1.0 / (1.0 + jnp.exp(-x))


def _bdot(a, b):
    return jnp.dot(a, b, preferred_element_type=jnp.float32)


def _resident(shape):
    return pl.BlockSpec(shape, lambda *_: (0,) * len(shape), pipeline_mode=pl.Buffered(1))


def _weight_scratch(shape):
    k, n = shape
    cap = WEIGHT_STAGE_BYTES // (4 * n)
    stage_rows = max(r for r in range(BF16_ROWS_PER_VREG, cap + 1, BF16_ROWS_PER_VREG)
                     if k % r == 0)
    return [pltpu.VMEM((k, n), jnp.bfloat16),
            pltpu.VMEM((WEIGHT_STAGE_SLOTS, stage_rows, n), jnp.float32),
            pltpu.SemaphoreType.DMA((WEIGHT_STAGE_SLOTS,))]


def _load_as_bf16(src_hbm, dst_ref, stage_ref, sem):
    chunk = stage_ref.shape[1]
    n = src_hbm.shape[0] // chunk
    assert n * chunk == src_hbm.shape[0] and stage_ref.shape[0] == WEIGHT_STAGE_SLOTS

    def copy(c):
        slot = c % WEIGHT_STAGE_SLOTS
        return pltpu.make_async_copy(src_hbm.at[pl.ds(c * chunk, chunk), :],
                                     stage_ref.at[slot], sem.at[slot])

    ahead = WEIGHT_STAGE_SLOTS - 1
    for c in range(min(ahead, n)):
        copy(c).start()
    for c in range(n):
        if c + ahead < n:
            copy(c + ahead).start()
        copy(c).wait()
        dst_ref[c * chunk:(c + 1) * chunk, :] = stage_ref[c % WEIGHT_STAGE_SLOTS].astype(
            jnp.bfloat16)


def _ffn_kernel(x_ref, g_in_ref, g_out_ref, w_in_hbm, w_out_hbm, o_ref, act_ref,
                w_in_ref, stage_in_ref, sem_in, w_out_ref, stage_out_ref, sem_out,
                *, d_ff, layer, which):
    @pl.when(pl.program_id(0) == 0)
    def _():
        _load_as_bf16(w_in_hbm.at[layer, which], w_in_ref, stage_in_ref, sem_in)
        _load_as_bf16(w_out_hbm.at[layer, which], w_out_ref, stage_out_ref, sem_out)

    x = x_ref[...]
    xn = _rms(x, g_in_ref[...]).astype(jnp.bfloat16)
    for c0 in range(0, d_ff, FFN_COLS):
        gate = _bdot(xn, w_in_ref[:, c0:c0 + FFN_COLS])
        up = _bdot(xn, w_in_ref[:, d_ff + c0:d_ff + c0 + FFN_COLS])
        act_ref[:, c0:c0 + FFN_COLS] = (gate * _sigmoid(gate) * up).astype(jnp.bfloat16)
    for r0 in range(0, x.shape[0], FFN_OUT_ROWS):
        rs = slice(r0, r0 + FFN_OUT_ROWS)
        h = _bdot(act_ref[rs, :], w_out_ref[...])
        o_ref[rs, :] = x[rs] + 0.5 * _rms(h, g_out_ref[...])


def _ffn(x2d, g_in, g_out, w_in_all, w_out_all, *, layer, which):
    m, d = x2d.shape
    d_ff = w_out_all.shape[2]
    assert m % FFN_ROWS == 0 and d_ff % FFN_COLS == 0
    row = pl.BlockSpec((FFN_ROWS, d), lambda i: (i, 0))
    hbm = pl.BlockSpec(memory_space=pl.ANY)
    return pl.pallas_call(
        functools.partial(_ffn_kernel, d_ff=d_ff, layer=layer, which=which),
        grid=(m // FFN_ROWS,),
        in_specs=[row, _resident((1, d)), _resident((1, d)), hbm, hbm],
        out_specs=row,
        out_shape=jax.ShapeDtypeStruct((m, d), jnp.float32),
        scratch_shapes=[pltpu.VMEM((FFN_ROWS, d_ff), jnp.bfloat16)]
        + _weight_scratch(w_in_all.shape[2:]) + _weight_scratch(w_out_all.shape[2:]),
        compiler_params=pltpu.CompilerParams(
            dimension_semantics=("arbitrary",), vmem_limit_bytes=VMEM_LIMIT_BYTES),
        name="ffn",
    )(x2d, g_in, g_out, w_in_all, w_out_all)


def _conv_kernel(x_ref, g_pre_ref, g_post_ref, w_in_hbm, b_in_ref, w_dw_ref, b_dw_ref,
                 ln_g_ref, ln_b_ref, w_out_hbm, b_out_ref, o_ref,
                 hext_ref, shift_ref, taps_ref, conv_ref,
                 w_in_ref, stage_in_ref, sem_in, w_out_ref, stage_out_ref, sem_out, *, idx):
    d = x_ref.shape[-1]
    rows = x_ref.shape[1]
    n_ext = CONV_HALO + rows
    first = pl.program_id(1) == 0

    @pl.when(jnp.logical_and(pl.program_id(0) == 0, first))
    def _():
        _load_as_bf16(w_in_hbm.at[idx], w_in_ref, stage_in_ref, sem_in)
        _load_as_bf16(w_out_hbm.at[idx], w_out_ref, stage_out_ref, sem_out)

    @pl.when(first)
    def _():
        hext_ref[0:CONV_HALO, :] = jnp.zeros((CONV_HALO, d), jnp.float32)
        hext_ref[n_ext:n_ext + V7X_SUBLANES, :] = jnp.zeros((V7X_SUBLANES, d), jnp.float32)

    @pl.when(jnp.logical_not(first))
    def _():
        hext_ref[0:CONV_HALO, :] = hext_ref[rows:n_ext, :]

    x = x_ref[0]
    xn = _rms(x, g_pre_ref[...]).astype(jnp.bfloat16)
    for j in range(CONV_WIDTH):
        taps_ref[j] = jnp.broadcast_to(w_dw_ref[j:j + 1, :], (V7X_SUBLANES, d))

    sub = lax.broadcasted_iota(jnp.int32, (V7X_SUBLANES, CONV_LANES), 0)
    n_tiles = n_ext // V7X_SUBLANES
    for c0 in range(0, d, CONV_LANES):
        lanes = slice(c0, c0 + CONV_LANES)
        gate_lanes = slice(d + c0, d + c0 + CONV_LANES)
        a = _bdot(xn, w_in_ref[:, lanes]) + b_in_ref[:, lanes]
        gate = _bdot(xn, w_in_ref[:, gate_lanes]) + b_in_ref[:, gate_lanes]
        hext_ref[CONV_HALO:n_ext, lanes] = a * _sigmoid(gate)

        for s in range(1, V7X_SUBLANES):
            rot = [pltpu.roll(hext_ref[i * V7X_SUBLANES:(i + 1) * V7X_SUBLANES, lanes],
                              V7X_SUBLANES - s, axis=0) for i in range(n_tiles + 1)]
            for i in range(n_tiles):
                shift_ref[s - 1, i * V7X_SUBLANES:(i + 1) * V7X_SUBLANES, lanes] = jnp.where(
                    sub < V7X_SUBLANES - s, rot[i], rot[i + 1])

        for r0 in range(0, rows, CONV_SUB):
            acc = [jnp.zeros((V7X_SUBLANES, CONV_LANES), jnp.float32)
                   for _ in range(CONV_SUB // V7X_SUBLANES)]
            for j in range(CONV_WIDTH):
                off = CONV_HALO - (CONV_WIDTH - 1) + j
                lo = r0 + off - off % V7X_SUBLANES
                tap = taps_ref[j, :, c0:c0 + CONV_LANES]
                for i in range(len(acc)):
                    r = lo + i * V7X_SUBLANES
                    if off % V7X_SUBLANES == 0:
                        win = hext_ref[r:r + V7X_SUBLANES, c0:c0 + CONV_LANES]
                    else:
                        win = shift_ref[off % V7X_SUBLANES - 1, r:r + V7X_SUBLANES,
                                        c0:c0 + CONV_LANES]
                    acc[i] = acc[i] + tap * win
            for i in range(len(acc)):
                r = r0 + i * V7X_SUBLANES
                conv_ref[r:r + V7X_SUBLANES, c0:c0 + CONV_LANES] = acc[i]

    h = conv_ref[...] + b_dw_ref[...]
    mu = jnp.mean(h, axis=-1, keepdims=True)
    hc = h - mu
    hn = hc * lax.rsqrt(jnp.mean(hc * hc, axis=-1, keepdims=True) + EPS)
    hn = hn * ln_g_ref[...] + ln_b_ref[...]
    act = (hn * _sigmoid(hn)).astype(jnp.bfloat16)
    m = _bdot(act, w_out_ref[...]) + b_out_ref[...]
    o_ref[0] = x + _rms(m, g_post_ref[...])


def _conv_mixer(x, g_pre, g_post, w_in_all, b_in, w_dw, b_dw, ln_g, ln_b, w_out_all, b_out, *, idx):
    bsz, seq, d = x.shape
    assert seq % CONV_ROWS == 0 and CONV_ROWS % CONV_SUB == 0
    assert CONV_HALO >= CONV_WIDTH - 1 and CONV_HALO % V7X_SUBLANES == 0
    n_ext = CONV_HALO + CONV_ROWS
    row = pl.BlockSpec((1, CONV_ROWS, d), lambda b, s: (b, s, 0))
    vec = _resident((1, d))
    hbm = pl.BlockSpec(memory_space=pl.ANY)
    return pl.pallas_call(
        functools.partial(_conv_kernel, idx=idx),
        grid=(bsz, seq // CONV_ROWS),
        in_specs=[row, vec, vec, hbm, _resident((1, 2 * d)),
                  _resident(w_dw.shape), vec, vec, vec, hbm, vec],
        out_specs=row,
        out_shape=jax.ShapeDtypeStruct(x.shape, jnp.float32),
        scratch_shapes=[pltpu.VMEM((n_ext + V7X_SUBLANES, d), jnp.float32),
                        pltpu.VMEM((V7X_SUBLANES - 1, n_ext, d), jnp.float32),
                        pltpu.VMEM((CONV_WIDTH, V7X_SUBLANES, d), jnp.float32),
                        pltpu.VMEM((CONV_ROWS, d), jnp.float32)]
        + _weight_scratch(w_in_all.shape[1:]) + _weight_scratch(w_out_all.shape[1:]),
        compiler_params=pltpu.CompilerParams(
            dimension_semantics=("arbitrary", "arbitrary"), vmem_limit_bytes=VMEM_LIMIT_BYTES),
        name="conv_mixer",
    )(x, g_pre, g_post, w_in_all, b_in, w_dw, b_dw, ln_g, ln_b, w_out_all, b_out)


HGRN_LEVELS = tuple(HGRN_CHUNK >> (i + 1) for i in range(HGRN_CHUNK.bit_length() - 1))
HGRN_MXU_LEVELS = tuple(h for h in HGRN_LEVELS if 2 * h < V7X_SUBLANES)
LOG2_E = 1.4426950408889634


def _hgrn_sum_matrix():
    c = HGRN_CHUNK
    t = np.arange(c)[:, None]
    u = np.arange(c)[None, :]
    blocks = [u <= t]
    for h in HGRN_MXU_LEVELS:
        mid = (t // (2 * h)) * (2 * h) + h
        upper = t >= mid
        blocks.append(np.where(upper, (u >= mid) & (u <= t), (u > t) & (u < mid)))
    return np.concatenate(blocks, axis=0).astype(np.float32)


def _split3(x):
    hi = x.astype(jnp.bfloat16)
    r1 = x - hi.astype(jnp.float32)
    mid = r1.astype(jnp.bfloat16)
    lo = (r1 - mid.astype(jnp.float32)).astype(jnp.bfloat16)
    return hi, mid, lo


def _dot_nt(a, b):
    return lax.dot_general(a, b, (((1,), (1,)), ((), ())), preferred_element_type=jnp.float32)


def _dot_tn(a, b):
    return lax.dot_general(a, b, (((0,), (0,)), ((), ())), preferred_element_type=jnp.float32)


def _hgrn_kernel(x_ref, g_pre_ref, g_post_ref, w_in_hbm, lb_logits_ref, g_norm_ref, w_out_hbm,
                 sum_ref, o_ref, state_ref,
                 w_in_ref, stage_in_ref, sem_in, w_out_ref, stage_out_ref, sem_out, *, layer, idx):
    d = x_ref.shape[-1]
    rows = x_ref.shape[1]
    c = HGRN_CHUNK
    sub = HGRN_SUB
    n_sub = rows // sub
    chunks_per_sub = sub // c
    grp = HGRN_GROUP * c
    heads = [slice(hd * HEAD_DIM, (hd + 1) * HEAD_DIM) for hd in range(HGRN_HEADS)]

    @pl.when(jnp.logical_and(pl.program_id(0) == 0, pl.program_id(1) == 0))
    def _():
        _load_as_bf16(w_in_hbm.at[idx], w_in_ref, stage_in_ref, sem_in)
        _load_as_bf16(w_out_hbm.at[idx], w_out_ref, stage_out_ref, sem_out)

    @pl.when(pl.program_id(1) == 0)
    def _():
        state_ref[...] = jnp.zeros(state_ref.shape, jnp.float32)

    logits = lb_logits_ref[...]
    e = jnp.exp(logits - jnp.max(logits, axis=0, keepdims=True))
    p = e / jnp.sum(e, axis=0, keepdims=True)
    cum = p[0:1]
    for l in range(1, layer + 1):
        cum = cum + p[l:l + 1]
    lb = cum - p[0:1]

    tt = lax.broadcasted_iota(jnp.int32, (grp, grp), 0)
    ss = lax.broadcasted_iota(jnp.int32, (grp, grp), 1)
    diag_mask = tt == ss
    masks = []
    for h in HGRN_LEVELS:
        shift = (2 * h).bit_length() - 1
        masks.append(((tt >> shift) == (ss >> shift)) & ((tt & h) != 0) & ((ss & h) == 0))

    def project(si):
        x = x_ref[0, si * sub:(si + 1) * sub, :]
        xn = _rms(x, g_pre_ref[...]).astype(jnp.bfloat16)
        f = _bdot(xn, w_in_ref[:, d:2 * d])
        q = _bdot(xn, w_in_ref[:, 0:d])
        v = _bdot(xn, w_in_ref[:, 2 * d:3 * d]).astype(jnp.bfloat16)
        gate = _bdot(xn, w_in_ref[:, 3 * d:4 * d])
        q = q * _sigmoid(q)
        t = jnp.exp(-jnp.abs(f))
        w = 1.0 + t
        r = 1.0 / w
        a1 = jnp.log(lb)
        a2 = jnp.log1p(-lb) + (jnp.minimum(f, 0.0) - jnp.log(w))
        log_f = jnp.maximum(a1, a2) + jnp.log(1.0 + jnp.exp(-jnp.abs(a1 - a2)))
        k = (1.0 - lb) * jnp.where(f >= 0.0, t * r, r)
        return dict(x=x, q=q, k=k, v=v, lf2=log_f * LOG2_E, gate=gate)

    def exponents(t, ci):
        hi, mid, lo = _split3(t["lf2"][ci * c:(ci + 1) * c])
        return _bdot(sum_ref[...], jnp.concatenate([hi, mid, lo], axis=0))

    def chunk_factors(t, ci, ex):
        qc = t["q"][ci * c:(ci + 1) * c]
        kc = t["k"][ci * c:(ci + 1) * c]
        b = ex[0:c]
        b_last = b[c - 1:c]
        q_dec = (qc * jnp.exp2(b)).astype(jnp.bfloat16)
        k_dec = (kc * jnp.exp2(b_last - b)).astype(jnp.bfloat16)
        q_bf, k_bf = qc.astype(jnp.bfloat16), kc.astype(jnp.bfloat16)
        q_lv, k_lv = [], []
        mxu_block = 1
        for h in HGRN_LEVELS:
            if h in HGRN_MXU_LEVELS:
                expo = ex[mxu_block * c:(mxu_block + 1) * c]
                mxu_block += 1
            else:
                ref = jnp.concatenate(
                    [jnp.broadcast_to(b[g + h - 1:g + h], (2 * h, d)) for g in range(0, c, 2 * h)],
                    axis=0)
                expo = -jnp.abs(b - ref)
            fac = jnp.exp2(expo).astype(jnp.bfloat16)
            q_lv.append(q_bf * fac)
            k_lv.append(k_bf * fac)
        return dict(q_lv=q_lv, k_lv=k_lv, qk=qc * kc, q_dec=q_dec, k_dec=k_dec,
                    decay_last=jnp.exp2(b_last))

    def group_scores(chunks):
        rows_of = lambda key, l=None: jnp.concatenate(
            [ch[key] if l is None else ch[key][l] for ch in chunks], axis=0)
        qk = rows_of("qk")
        q_lv = [rows_of("q_lv", l) for l in range(len(HGRN_LEVELS))]
        k_lv = [rows_of("k_lv", l) for l in range(len(HGRN_LEVELS))]
        scores = []
        for sl in heads:
            acc = jnp.where(diag_mask, jnp.sum(qk[:, sl], axis=-1, keepdims=True), 0.0)
            for ql, kl, mask in zip(q_lv, k_lv, masks):
                acc = acc + jnp.where(mask, _dot_nt(ql[:, sl], kl[:, sl]), 0.0)
            scores.append(acc.astype(jnp.bfloat16))
        return scores

    def state_path(t, ci, ch):
        vc = t["v"][ci * c:(ci + 1) * c]
        outs = []
        for hd, sl in enumerate(heads):
            state_t = state_ref[hd]
            outs.append(_dot_nt(ch["q_dec"][:, sl], state_t.astype(jnp.bfloat16)))
            state_ref[hd] = (state_t * ch["decay_last"][:, sl]
                             + _dot_tn(vc[:, sl], ch["k_dec"][:, sl]))
        return outs

    def finish(si, t, o):
        gate = t["gate"] * _sigmoid(t["gate"])
        parts = []
        for sl in heads:
            oh = o[:, sl]
            parts.append(oh * lax.rsqrt(jnp.mean(oh * oh, axis=-1, keepdims=True) + EPS))
        on = jnp.concatenate(parts, axis=-1) * g_norm_ref[...] * gate
        m = _bdot(on.astype(jnp.bfloat16), w_out_ref[...])
        o_ref[0, si * sub:(si + 1) * sub, :] = t["x"] + _rms(m, g_post_ref[...])

    tiles = [project(si) for si in range(n_sub)]
    pending = None
    for si, t in enumerate(tiles):
        exs = [exponents(t, ci) for ci in range(chunks_per_sub)]
        if pending is not None:
            finish(*pending)
        chunks = [chunk_factors(t, ci, exs[ci]) for ci in range(chunks_per_sub)]
        scores = [group_scores(chunks[g:g + HGRN_GROUP])
                  for g in range(0, chunks_per_sub, HGRN_GROUP)]
        o_inter = [state_path(t, ci, ch) for ci, ch in enumerate(chunks)]
        o_rows = []
        for gi, sc in enumerate(scores):
            vg = t["v"][gi * grp:(gi + 1) * grp]
            inter = [jnp.concatenate([o_inter[gi * HGRN_GROUP + j][hd] for j in range(HGRN_GROUP)],
                                     axis=0) for hd in range(HGRN_HEADS)]
            o_rows.append(jnp.concatenate(
                [inter[hd] + _bdot(sc[hd], vg[:, sl]) for hd, sl in enumerate(heads)], axis=-1))
        pending = (si, t, jnp.concatenate(o_rows, axis=0))
    finish(*pending)


def _hgrn_mixer(x, g_pre, g_post, w_in_all, lb_logits, g_norm, w_out_all, *, layer, idx):
    bsz, seq, d = x.shape
    assert seq % HGRN_ROWS == 0 and HGRN_ROWS % HGRN_SUB == 0
    assert HGRN_SUB % (HGRN_GROUP * HGRN_CHUNK) == 0
    assert d == HGRN_HEADS * HEAD_DIM and w_in_all.shape[1:] == (d, 4 * d)
    sum_mat = _hgrn_sum_matrix()
    sum3 = jnp.asarray(np.concatenate([sum_mat] * 3, axis=1), jnp.bfloat16)
    g_norm_row = jnp.tile(g_norm.reshape(1, HEAD_DIM), (1, HGRN_HEADS))
    row = pl.BlockSpec((1, HGRN_ROWS, d), lambda b, s: (b, s, 0))
    vec = _resident((1, d))
    hbm = pl.BlockSpec(memory_space=pl.ANY)
    return pl.pallas_call(
        functools.partial(_hgrn_kernel, layer=layer, idx=idx),
        grid=(bsz, seq // HGRN_ROWS),
        in_specs=[row, vec, vec, hbm, _resident(lb_logits.shape), vec, hbm,
                  _resident(sum3.shape)],
        out_specs=row,
        out_shape=jax.ShapeDtypeStruct(x.shape, jnp.float32),
        scratch_shapes=[pltpu.VMEM((HGRN_HEADS, HEAD_DIM, HEAD_DIM), jnp.float32)]
        + _weight_scratch(w_in_all.shape[1:]) + _weight_scratch(w_out_all.shape[1:]),
        compiler_params=pltpu.CompilerParams(
            dimension_semantics=("arbitrary", "arbitrary"), vmem_limit_bytes=VMEM_LIMIT_BYTES),
        name="hgrn_mixer",
    )(x, g_pre, g_post, w_in_all, lb_logits, g_norm_row, w_out_all, sum3)


def kernel(x, norm_gains, ffn_w_in, ffn_w_out, conv_w_in, conv_b_in, conv_w_dw, conv_b_dw,
           conv_ln_g, conv_ln_b, conv_w_out, conv_b_out, hgrn_w_in, hgrn_lb_logits,
           hgrn_g_norm, hgrn_w_out):
    bsz, seq, d = x.shape
    depth = norm_gains.shape[0]
    vec = lambda v: v.reshape(1, -1).astype(jnp.float32)

    def ffn(x, layer, which):
        g = norm_gains[layer]
        y = _ffn(x.reshape(bsz * seq, d), vec(g[4 * which]), vec(g[4 * which + 1]),
                 ffn_w_in, ffn_w_out, layer=layer, which=which)
        return y.reshape(bsz, seq, d)

    for layer in range(depth):
        g = norm_gains[layer]
        x = ffn(x, layer, 0)
        j = layer // 2
        if layer % 2 == 0:
            x = _conv_mixer(x, vec(g[2]), vec(g[3]), conv_w_in, vec(conv_b_in[j]),
                            conv_w_dw[j], vec(conv_b_dw[j]), vec(conv_ln_g[j]),
                            vec(conv_ln_b[j]), conv_w_out, vec(conv_b_out[j]), idx=j)
        else:
            x = _hgrn_mixer(x, vec(g[2]), vec(g[3]), hgrn_w_in, hgrn_lb_logits,
                            hgrn_g_norm[j], hgrn_w_out, layer=layer, idx=j)
        x = ffn(x, layer, 1)
    return x
```

```python
import functools

import jax
import jax.numpy as jnp
import numpy as np
from jax import lax
from jax.experimental import pallas as pl
from jax.experimental.pallas import tpu as pltpu

EPS = 1e-6
CONV_WIDTH = 31
HGRN_HEADS = 8
HEAD_DIM = 128

V7X_SUBLANES = 8
V7X_VMEM_BYTES = 64 * 1024 * 1024
VMEM_LIMIT_BYTES = 56 * 1024 * 1024

FFN_ROWS = 1024
FFN_COLS = 256
FFN_OUT_ROWS = 512
WEIGHT_STAGE_SLOTS = 4
WEIGHT_STAGE_BYTES = 1024 * 1024
BF16_ROWS_PER_VREG = 2 * V7X_SUBLANES
CONV_ROWS = 512
CONV_HALO = 32
CONV_SUB = 32
CONV_LANES = 256
HGRN_ROWS = 512
HGRN_SUB = 512
HGRN_CHUNK = 64
HGRN_GROUP = 2


def _rms(x, gain):
    return x * lax.rsqrt(jnp.mean(x * x, axis=-1, keepdims=True) + EPS) * gain


def _sigmoid(x):
    return 1.0 / (1.0 + jnp.exp(-x))


def _bdot(a, b):
    return jnp.dot(a, b, preferred_element_type=jnp.float32)


def _resident(shape):
    return pl.BlockSpec(shape, lambda *_: (0,) * len(shape), pipeline_mode=pl.Buffered(1))


def _weight_scratch(shape):
    k, n = shape
    cap = WEIGHT_STAGE_BYTES // (4 * n)
    stage_rows = max(r for r in range(BF16_ROWS_PER_VREG, cap + 1, BF16_ROWS_PER_VREG)
                     if k % r == 0)
    return [pltpu.VMEM((k, n), jnp.bfloat16),
            pltpu.VMEM((WEIGHT_STAGE_SLOTS, stage_rows, n), jnp.float32),
            pltpu.SemaphoreType.DMA((WEIGHT_STAGE_SLOTS,))]


def _load_as_bf16(src_hbm, dst_ref, stage_ref, sem):
    chunk = stage_ref.shape[1]
    n = src_hbm.shape[0] // chunk
    assert n * chunk == src_hbm.shape[0] and stage_ref.shape[0] == WEIGHT_STAGE_SLOTS

    def copy(c):
        slot = c % WEIGHT_STAGE_SLOTS
        return pltpu.make_async_copy(src_hbm.at[pl.ds(c * chunk, chunk), :],
                                     stage_ref.at[slot], sem.at[slot])

    ahead = WEIGHT_STAGE_SLOTS - 1
    for c in range(min(ahead, n)):
        copy(c).start()
    for c in range(n):
        if c + ahead < n:
            copy(c + ahead).start()
        copy(c).wait()
        dst_ref[c * chunk:(c + 1) * chunk, :] = stage_ref[c % WEIGHT_STAGE_SLOTS].astype(
            jnp.bfloat16)


def _ffn_kernel(x_ref, g_in_ref, g_out_ref, w_in_hbm, w_out_hbm, o_ref, act_ref,
                w_in_ref, stage_in_ref, sem_in, w_out_ref, stage_out_ref, sem_out,
                *, d_ff, layer, which):
    @pl.when(pl.program_id(0) == 0)
    def _():
        _load_as_bf16(w_in_hbm.at[layer, which], w_in_ref, stage_in_ref, sem_in)
        _load_as_bf16(w_out_hbm.at[layer, which], w_out_ref, stage_out_ref, sem_out)

    x = x_ref[...]
    xn = _rms(x, g_in_ref[...]).astype(jnp.bfloat16)
    for c0 in range(0, d_ff, FFN_COLS):
        gate = _bdot(xn, w_in_ref[:, c0:c0 + FFN_COLS])
        up = _bdot(xn, w_in_ref[:, d_ff + c0:d_ff + c0 + FFN_COLS])
        act_ref[:, c0:c0 + FFN_COLS] = (gate * _sigmoid(gate) * up).astype(jnp.bfloat16)
    for r0 in range(0, x.shape[0], FFN_OUT_ROWS):
        rs = slice(r0, r0 + FFN_OUT_ROWS)
        h = _bdot(act_ref[rs, :], w_out_ref[...])
        o_ref[rs, :] = x[rs] + 0.5 * _rms(h, g_out_ref[...])


def _ffn(x2d, g_in, g_out, w_in_all, w_out_all, *, layer, which):
    m, d = x2d.shape
    d_ff = w_out_all.shape[2]
    assert m % FFN_ROWS == 0 and d_ff % FFN_COLS == 0
    row = pl.BlockSpec((FFN_ROWS, d), lambda i: (i, 0))
    hbm = pl.BlockSpec(memory_space=pl.ANY)
    return pl.pallas_call(
        functools.partial(_ffn_kernel, d_ff=d_ff, layer=layer, which=which),
        grid=(m // FFN_ROWS,),
        in_specs=[row, _resident((1, d)), _resident((1, d)), hbm, hbm],
        out_specs=row,
        out_shape=jax.ShapeDtypeStruct((m, d), jnp.float32),
        scratch_shapes=[pltpu.VMEM((FFN_ROWS, d_ff), jnp.bfloat16)]
        + _weight_scratch(w_in_all.shape[2:]) + _weight_scratch(w_out_all.shape[2:]),
        compiler_params=pltpu.CompilerParams(
            dimension_semantics=("arbitrary",), vmem_limit_bytes=VMEM_LIMIT_BYTES),
        name="ffn",
    )(x2d, g_in, g_out, w_in_all, w_out_all)


def _conv_kernel(x_ref, g_pre_ref, g_post_ref, w_in_hbm, b_in_ref, w_dw_ref, b_dw_ref,
                 ln_g_ref, ln_b_ref, w_out_hbm, b_out_ref, o_ref,
                 hext_ref, shift_ref, taps_ref, conv_ref,
                 w_in_ref, stage_in_ref, sem_in, w_out_ref, stage_out_ref, sem_out, *, idx):
    d = x_ref.shape[-1]
    rows = x_ref.shape[1]
    n_ext = CONV_HALO + rows
    first = pl.program_id(1) == 0

    @pl.when(jnp.logical_and(pl.program_id(0) == 0, first))
    def _():
        _load_as_bf16(w_in_hbm.at[idx], w_in_ref, stage_in_ref, sem_in)
        _load_as_bf16(w_out_hbm.at[idx], w_out_ref, stage_out_ref, sem_out)

    @pl.when(first)
    def _():
        hext_ref[0:CONV_HALO, :] = jnp.zeros((CONV_HALO, d), jnp.float32)
        hext_ref[n_ext:n_ext + V7X_SUBLANES, :] = jnp.zeros((V7X_SUBLANES, d), jnp.float32)

    @pl.when(jnp.logical_not(first))
    def _():
        hext_ref[0:CONV_HALO, :] = hext_ref[rows:n_ext, :]

    x = x_ref[0]
    xn = _rms(x, g_pre_ref[...]).astype(jnp.bfloat16)
    for j in range(CONV_WIDTH):
        taps_ref[j] = jnp.broadcast_to(w_dw_ref[j:j + 1, :], (V7X_SUBLANES, d))

    sub = lax.broadcasted_iota(jnp.int32, (V7X_SUBLANES, CONV_LANES), 0)
    n_tiles = n_ext // V7X_SUBLANES
    for c0 in range(0, d, CONV_LANES):
        lanes = slice(c0, c0 + CONV_LANES)
        gate_lanes = slice(d + c0, d + c0 + CONV_LANES)
        a = _bdot(xn, w_in_ref[:, lanes]) + b_in_ref[:, lanes]
        gate = _bdot(xn, w_in_ref[:, gate_lanes]) + b_in_ref[:, gate_lanes]
        hext_ref[CONV_HALO:n_ext, lanes] = a * _sigmoid(gate)

        for s in range(1, V7X_SUBLANES):
            rot = [pltpu.roll(hext_ref[i * V7X_SUBLANES:(i + 1) * V7X_SUBLANES, lanes],
                              V7X_SUBLANES - s, axis=0) for i in range(n_tiles + 1)]
            for i in range(n_tiles):
                shift_ref[s - 1, i * V7X_SUBLANES:(i + 1) * V7X_SUBLANES, lanes] = jnp.where(
                    sub < V7X_SUBLANES - s, rot[i], rot[i + 1])

        for r0 in range(0, rows, CONV_SUB):
            acc = [jnp.zeros((V7X_SUBLANES, CONV_LANES), jnp.float32)
                   for _ in range(CONV_SUB // V7X_SUBLANES)]
            for j in range(CONV_WIDTH):
                off = CONV_HALO - (CONV_WIDTH - 1) + j
                lo = r0 + off - off % V7X_SUBLANES
                tap = taps_ref[j, :, c0:c0 + CONV_LANES]
                for i in range(len(acc)):
                    r = lo + i * V7X_SUBLANES
                    if off % V7X_SUBLANES == 0:
                        win = hext_ref[r:r + V7X_SUBLANES, c0:c0 + CONV_LANES]
                    else:
                        win = shift_ref[off % V7X_SUBLANES - 1, r:r + V7X_SUBLANES,
                                        c0:c0 + CONV_LANES]
                    acc[i] = acc[i] + tap * win
            for i in range(len(acc)):
                r = r0 + i * V7X_SUBLANES
                conv_ref[r:r + V7X_SUBLANES, c0:c0 + CONV_LANES] = acc[i]

    h = conv_ref[...] + b_dw_ref[...]
    mu = jnp.mean(h, axis=-1, keepdims=True)
    hc = h - mu
    hn = hc * lax.rsqrt(jnp.mean(hc * hc, axis=-1, keepdims=True) + EPS)
    hn = hn * ln_g_ref[...] + ln_b_ref[...]
    act = (hn * _sigmoid(hn)).astype(jnp.bfloat16)
    m = _bdot(act, w_out_ref[...]) + b_out_ref[...]
    o_ref[0] = x + _rms(m, g_post_ref[...])


def _conv_mixer(x, g_pre, g_post, w_in_all, b_in, w_dw, b_dw, ln_g, ln_b, w_out_all, b_out, *, idx):
    bsz, seq, d = x.shape
    assert seq % CONV_ROWS == 0 and CONV_ROWS % CONV_SUB == 0
    assert CONV_HALO >= CONV_WIDTH - 1 and CONV_HALO % V7X_SUBLANES == 0
    n_ext = CONV_HALO + CONV_ROWS
    row = pl.BlockSpec((1, CONV_ROWS, d), lambda b, s: (b, s, 0))
    vec = _resident((1, d))
    hbm = pl.BlockSpec(memory_space=pl.ANY)
    return pl.pallas_call(
        functools.partial(_conv_kernel, idx=idx),
        grid=(bsz, seq // CONV_ROWS),
        in_specs=[row, vec, vec, hbm, _resident((1, 2 * d)),
                  _resident(w_dw.shape), vec, vec, vec, hbm, vec],
        out_specs=row,
        out_shape=jax.ShapeDtypeStruct(x.shape, jnp.float32),
        scratch_shapes=[pltpu.VMEM((n_ext + V7X_SUBLANES, d), jnp.float32),
                        pltpu.VMEM((V7X_SUBLANES - 1, n_ext, d), jnp.float32),
                        pltpu.VMEM((CONV_WIDTH, V7X_SUBLANES, d), jnp.float32),
                        pltpu.VMEM((CONV_ROWS, d), jnp.float32)]
        + _weight_scratch(w_in_all.shape[1:]) + _weight_scratch(w_out_all.shape[1:]),
        compiler_params=pltpu.CompilerParams(
            dimension_semantics=("arbitrary", "arbitrary"), vmem_limit_bytes=VMEM_LIMIT_BYTES),
        name="conv_mixer",
    )(x, g_pre, g_post, w_in_all, b_in, w_dw, b_dw, ln_g, ln_b, w_out_all, b_out)


HGRN_LEVELS = tuple(HGRN_CHUNK >> (i + 1) for i in range(HGRN_CHUNK.bit_length() - 1))
HGRN_MXU_LEVELS = tuple(h for h in HGRN_LEVELS if 2 * h < V7X_SUBLANES)
LOG2_E = 1.4426950408889634


def _hgrn_sum_matrix():
    c = HGRN_CHUNK
    t = np.arange(c)[:, None]
    u = np.arange(c)[None, :]
    blocks = [u <= t]
    for h in HGRN_MXU_LEVELS:
        mid = (t // (2 * h)) * (2 * h) + h
        upper = t >= mid
        blocks.append(np.where(upper, (u >= mid) & (u <= t), (u > t) & (u < mid)))
    return np.concatenate(blocks, axis=0).astype(np.float32)


def _split3(x):
    hi = x.astype(jnp.bfloat16)
    r1 = x - hi.astype(jnp.float32)
    mid = r1.astype(jnp.bfloat16)
    lo = (r1 - mid.astype(jnp.float32)).astype(jnp.bfloat16)
    return hi, mid, lo


def _dot_nt(a, b):
    return lax.dot_general(a, b, (((1,), (1,)), ((), ())), preferred_element_type=jnp.float32)


def _dot_tn(a, b):
    return lax.dot_general(a, b, (((0,), (0,)), ((), ())), preferred_element_type=jnp.float32)


def _hgrn_kernel(x_ref, g_pre_ref, g_post_ref, w_in_hbm, lb_logits_ref, g_norm_ref, w_out_hbm,
                 sum_ref, o_ref, state_ref,
                 w_in_ref, stage_in_ref, sem_in, w_out_ref, stage_out_ref, sem_out, *, layer, idx):
    d = x_ref.shape[-1]
    rows = x_ref.shape[1]
    c = HGRN_CHUNK
    sub = HGRN_SUB
    n_sub = rows // sub
    chunks_per_sub = sub // c
    grp = HGRN_GROUP * c
    heads = [slice(hd * HEAD_DIM, (hd + 1) * HEAD_DIM) for hd in range(HGRN_HEADS)]

    @pl.when(jnp.logical_and(pl.program_id(0) == 0, pl.program_id(1) == 0))
    def _():
        _load_as_bf16(w_in_hbm.at[idx], w_in_ref, stage_in_ref, sem_in)
        _load_as_bf16(w_out_hbm.at[idx], w_out_ref, stage_out_ref, sem_out)

    @pl.when(pl.program_id(1) == 0)
    def _():
        state_ref[...] = jnp.zeros(state_ref.shape, jnp.float32)

    logits = lb_logits_ref[...]
    e = jnp.exp(logits - jnp.max(logits, axis=0, keepdims=True))
    p = e / jnp.sum(e, axis=0, keepdims=True)
    cum = p[0:1]
    for l in range(1, layer + 1):
        cum = cum + p[l:l + 1]
    lb = cum - p[0:1]

    tt = lax.broadcasted_iota(jnp.int32, (grp, grp), 0)
    ss = lax.broadcasted_iota(jnp.int32, (grp, grp), 1)
    diag_mask = tt == ss
    masks = []
    for h in HGRN_LEVELS:
        shift = (2 * h).bit_length() - 1
        masks.append(((tt >> shift) == (ss >> shift)) & ((tt & h) != 0) & ((ss & h) == 0))

    def project(si):
        x = x_ref[0, si * sub:(si + 1) * sub, :]
        xn = _rms(x, g_pre_ref[...]).astype(jnp.bfloat16)
        f = _bdot(xn, w_in_ref[:, d:2 * d])
        q = _bdot(xn, w_in_ref[:, 0:d])
        v = _bdot(xn, w_in_ref[:, 2 * d:3 * d]).astype(jnp.bfloat16)
        gate = _bdot(xn, w_in_ref[:, 3 * d:4 * d])
        q = q * _sigmoid(q)
        t = jnp.exp(-jnp.abs(f))
        w = 1.0 + t
        r = 1.0 / w
        a1 = jnp.log(lb)
        a2 = jnp.log1p(-lb) + (jnp.minimum(f, 0.0) - jnp.log(w))
        log_f = jnp.maximum(a1, a2) + jnp.log(1.0 + jnp.exp(-jnp.abs(a1 - a2)))
        k = (1.0 - lb) * jnp.where(f >= 0.0, t * r, r)
        return dict(x=x, q=q, k=k, v=v, lf2=log_f * LOG2_E, gate=gate)

    def exponents(t, ci):
        hi, mid, lo = _split3(t["lf2"][ci * c:(ci + 1) * c])
        return _bdot(sum_ref[...], jnp.concatenate([hi, mid, lo], axis=0))

    def chunk_factors(t, ci, ex):
        qc = t["q"][ci * c:(ci + 1) * c]
        kc = t["k"][ci * c:(ci + 1) * c]
        b = ex[0:c]
        b_last = b[c - 1:c]
        q_dec = (qc * jnp.exp2(b)).astype(jnp.bfloat16)
        k_dec = (kc * jnp.exp2(b_last - b)).astype(jnp.bfloat16)
        q_bf, k_bf = qc.astype(jnp.bfloat16), kc.astype(jnp.bfloat16)
        q_lv, k_lv = [], []
        mxu_block = 1
        for h in HGRN_LEVELS:
            if h in HGRN_MXU_LEVELS:
                expo = ex[mxu_block * c:(mxu_block + 1) * c]
                mxu_block += 1
            else:
                pieces = []
                for g in range(0, c, 2 * h):
                    ref = b[g + h - 1:g + h]
                    if h >= V7X_SUBLANES:
                        pieces += [ref - b[g:g + h], b[g + h:g + 2 * h] - ref]
                    else:
                        pieces.append(-jnp.abs(b[g:g + 2 * h] - ref))
                expo = jnp.concatenate(pieces, axis=0)
            fac = jnp.exp2(expo).astype(jnp.bfloat16)
            q_lv.append(q_bf * fac)
            k_lv.append(k_bf * fac)
        return dict(q_lv=q_lv, k_lv=k_lv, qk=qc * kc, q_dec=q_dec, k_dec=k_dec,
                    decay_last=jnp.exp2(b_last))

    def group_scores(chunks):
        rows_of = lambda key, l=None: jnp.concatenate(
            [ch[key] if l is None else ch[key][l] for ch in chunks], axis=0)
        qk = rows_of("qk")
        q_lv = [rows_of("q_lv", l) for l in range(len(HGRN_LEVELS))]
        k_lv = [rows_of("k_lv", l) for l in range(len(HGRN_LEVELS))]
        scores = []
        for sl in heads:
            acc = jnp.where(diag_mask, jnp.sum(qk[:, sl], axis=-1, keepdims=True), 0.0)
            for ql, kl, mask in zip(q_lv, k_lv, masks):
                acc = jnp.where(mask, _dot_nt(ql[:, sl], kl[:, sl]), acc)
            scores.append(acc.astype(jnp.bfloat16))
        return scores

    def state_path(t, ci, ch):
        vc = t["v"][ci * c:(ci + 1) * c]
        outs = []
        for hd, sl in enumerate(heads):
            state_t = state_ref[hd]
            outs.append(_dot_nt(ch["q_dec"][:, sl], state_t.astype(jnp.bfloat16)))
            state_ref[hd] = (state_t * ch["decay_last"][:, sl]
                             + _dot_tn(vc[:, sl], ch["k_dec"][:, sl]))
        return outs

    def finish(si, t, o):
        gate = t["gate"] * _sigmoid(t["gate"])
        parts = []
        for sl in heads:
            oh = o[:, sl]
            parts.append(oh * lax.rsqrt(jnp.mean(oh * oh, axis=-1, keepdims=True) + EPS))
        on = jnp.concatenate(parts, axis=-1) * g_norm_ref[...] * gate
        m = _bdot(on.astype(jnp.bfloat16), w_out_ref[...])
        o_ref[0, si * sub:(si + 1) * sub, :] = t["x"] + _rms(m, g_post_ref[...])

    tiles = [project(si) for si in range(n_sub)]
    pending = None
    for si, t in enumerate(tiles):
        exs = [exponents(t, ci) for ci in range(chunks_per_sub)]
        if pending is not None:
            finish(*pending)
        chunks = [chunk_factors(t, ci, exs[ci]) for ci in range(chunks_per_sub)]
        scores = [group_scores(chunks[g:g + HGRN_GROUP])
                  for g in range(0, chunks_per_sub, HGRN_GROUP)]
        o_inter = [state_path(t, ci, ch) for ci, ch in enumerate(chunks)]
        o_rows = []
        for gi, sc in enumerate(scores):
            vg = t["v"][gi * grp:(gi + 1) * grp]
            inter = [jnp.concatenate([o_inter[gi * HGRN_GROUP + j][hd] for j in range(HGRN_GROUP)],
                                     axis=0) for hd in range(HGRN_HEADS)]
            o_rows.append(jnp.concatenate(
                [inter[hd] + _bdot(sc[hd], vg[:, sl]) for hd, sl in enumerate(heads)], axis=-1))
        pending = (si, t, jnp.concatenate(o_rows, axis=0))
    finish(*pending)


def _hgrn_mixer(x, g_pre, g_post, w_in_all, lb_logits, g_norm, w_out_all, *, layer, idx):
    bsz, seq, d = x.shape
    assert seq % HGRN_ROWS == 0 and HGRN_ROWS % HGRN_SUB == 0
    assert HGRN_SUB % (HGRN_GROUP * HGRN_CHUNK) == 0
    assert d == HGRN_HEADS * HEAD_DIM and w_in_all.shape[1:] == (d, 4 * d)
    sum_mat = _hgrn_sum_matrix()
    sum3 = jnp.asarray(np.concatenate([sum_mat] * 3, axis=1), jnp.bfloat16)
    g_norm_row = jnp.tile(g_norm.reshape(1, HEAD_DIM), (1, HGRN_HEADS))
    row = pl.BlockSpec((1, HGRN_ROWS, d), lambda b, s: (b, s, 0))
    vec = _resident((1, d))
    hbm = pl.BlockSpec(memory_space=pl.ANY)
    return pl.pallas_call(
        functools.partial(_hgrn_kernel, layer=layer, idx=idx),
        grid=(bsz, seq // HGRN_ROWS),
        in_specs=[row, vec, vec, hbm, _resident(lb_logits.shape), vec, hbm,
                  _resident(sum3.shape)],
        out_specs=row,
        out_shape=jax.ShapeDtypeStruct(x.shape, jnp.float32),
        scratch_shapes=[pltpu.VMEM((HGRN_HEADS, HEAD_DIM, HEAD_DIM), jnp.float32)]
        + _weight_scratch(w_in_all.shape[1:]) + _weight_scratch(w_out_all.shape[1:]),
        compiler_params=pltpu.CompilerParams(
            dimension_semantics=("arbitrary", "arbitrary"), vmem_limit_bytes=VMEM_LIMIT_BYTES),
        name="hgrn_mixer",
    )(x, g_pre, g_post, w_in_all, lb_logits, g_norm_row, w_out_all, sum3)


def kernel(x, norm_gains, ffn_w_in, ffn_w_out, conv_w_in, conv_b_in, conv_w_dw, conv_b_dw,
           conv_ln_g, conv_ln_b, conv_w_out, conv_b_out, hgrn_w_in, hgrn_lb_logits,
           hgrn_g_norm, hgrn_w_out):
    bsz, seq, d = x.shape
    depth = norm_gains.shape[0]
    vec = lambda v: v.reshape(1, -1).astype(jnp.float32)

    def ffn(x, layer, which):
        g = norm_gains[layer]
        y = _ffn(x.reshape(bsz * seq, d), vec(g[4 * which]), vec(g[4 * which + 1]),
                 ffn_w_in, ffn_w_out, layer=layer, which=which)
        return y.reshape(bsz, seq, d)

    for layer in range(depth):
        g = norm_gains[layer]
        x = ffn(x, layer, 0)
        j = layer // 2
        if layer % 2 == 0:
            x = _conv_mixer(x, vec(g[2]), vec(g[3]), conv_w_in, vec(conv_b_in[j]),
                            conv_w_dw[j], vec(conv_b_dw[j]), vec(conv_ln_g[j]),
                            vec(conv_ln_b[j]), conv_w_out, vec(conv_b_out[j]), idx=j)
        else:
            x = _hgrn_mixer(x, vec(g[2]), vec(g[3]), hgrn_w_in, hgrn_lb_logits,
                            hgrn_g_norm[j], hgrn_w_out, layer=layer, idx=j)
        x = ffn(x, layer, 1)
    return x
```

```python
import functools

import jax
import jax.numpy as jnp
import numpy as np
from jax import lax
from jax.experimental import pallas as pl
from jax.experimental.pallas import tpu as pltpu

EPS = 1e-6
CONV_WIDTH = 31
HGRN_HEADS = 8
HEAD_DIM = 128

V7X_SUBLANES = 8
V7X_VMEM_BYTES = 64 * 1024 * 1024
VMEM_LIMIT_BYTES = 56 * 1024 * 1024

FFN_ROWS = 1024
FFN_COLS = 256
FFN_OUT_ROWS = 512
WEIGHT_STAGE_SLOTS = 6
WEIGHT_STAGE_BYTES = 1024 * 1024
BF16_ROWS_PER_VREG = 2 * V7X_SUBLANES
CONV_ROWS = 512
CONV_HALO = 32
CONV_SUB = 32
CONV_LANES = 256
HGRN_ROWS = 512
HGRN_SUB = 512
HGRN_CHUNK = 64
HGRN_GROUP = 2


def _rms(x, gain):
    return x * lax.rsqrt(jnp.mean(x * x, axis=-1, keepdims=True) + EPS) * gain


def _sigmoid(x):
    return 1.0 / (1.0 + jnp.exp(-x))


def _bdot(a, b):
    return jnp.dot(a, b, preferred_element_type=jnp.float32)


def _resident(shape):
    return pl.BlockSpec(shape, lambda *_: (0,) * len(shape), pipeline_mode=pl.Buffered(1))


def _weight_scratch(shape):
    k, n = shape
    cap = WEIGHT_STAGE_BYTES // (4 * n)
    stage_rows = max(r for r in range(BF16_ROWS_PER_VREG, cap + 1, BF16_ROWS_PER_VREG)
                     if k % r == 0)
    return [pltpu.VMEM((k, n), jnp.bfloat16),
            pltpu.VMEM((WEIGHT_STAGE_SLOTS, stage_rows, n), jnp.float32),
            pltpu.SemaphoreType.DMA((WEIGHT_STAGE_SLOTS,))]


def _load_as_bf16(src_hbm, dst_ref, stage_ref, sem):
    chunk = stage_ref.shape[1]
    n = src_hbm.shape[0] // chunk
    assert n * chunk == src_hbm.shape[0] and stage_ref.shape[0] == WEIGHT_STAGE_SLOTS

    def copy(c):
        slot = c % WEIGHT_STAGE_SLOTS
        return pltpu.make_async_copy(src_hbm.at[pl.ds(c * chunk, chunk), :],
                                     stage_ref.at[slot], sem.at[slot])

    ahead = WEIGHT_STAGE_SLOTS - 1
    for c in range(min(ahead, n)):
        copy(c).start()
    for c in range(n):
        if c + ahead < n:
            copy(c + ahead).start()
        copy(c).wait()
        dst_ref[c * chunk:(c + 1) * chunk, :] = stage_ref[c % WEIGHT_STAGE_SLOTS].astype(
            jnp.bfloat16)


def _ffn_kernel(x_ref, g_in_ref, g_out_ref, w_in_hbm, w_out_hbm, o_ref, act_ref,
                w_in_ref, stage_in_ref, sem_in, w_out_ref, stage_out_ref, sem_out,
                *, d_ff, layer, which):
    @pl.when(pl.program_id(0) == 0)
    def _():
        _load_as_bf16(w_in_hbm.at[layer, which], w_in_ref, stage_in_ref, sem_in)
        _load_as_bf16(w_out_hbm.at[layer, which], w_out_ref, stage_out_ref, sem_out)

    x = x_ref[...]
    xn = _rms(x, g_in_ref[...]).astype(jnp.bfloat16)
    for c0 in range(0, d_ff, FFN_COLS):
        gate = _bdot(xn, w_in_ref[:, c0:c0 + FFN_COLS])
        up = _bdot(xn, w_in_ref[:, d_ff + c0:d_ff + c0 + FFN_COLS])
        act_ref[:, c0:c0 + FFN_COLS] = (gate * _sigmoid(gate) * up).astype(jnp.bfloat16)
    for r0 in range(0, x.shape[0], FFN_OUT_ROWS):
        rs = slice(r0, r0 + FFN_OUT_ROWS)
        h = _bdot(act_ref[rs, :], w_out_ref[...])
        o_ref[rs, :] = x[rs] + 0.5 * _rms(h, g_out_ref[...])


def _ffn(x2d, g_in, g_out, w_in_all, w_out_all, *, layer, which):
    m, d = x2d.shape
    d_ff = w_out_all.shape[2]
    assert m % FFN_ROWS == 0 and d_ff % FFN_COLS == 0
    row = pl.BlockSpec((FFN_ROWS, d), lambda i: (i, 0))
    hbm = pl.BlockSpec(memory_space=pl.ANY)
    return pl.pallas_call(
        functools.partial(_ffn_kernel, d_ff=d_ff, layer=layer, which=which),
        grid=(m // FFN_ROWS,),
        in_specs=[row, _resident((1, d)), _resident((1, d)), hbm, hbm],
        out_specs=row,
        out_shape=jax.ShapeDtypeStruct((m, d), jnp.float32),
        scratch_shapes=[pltpu.VMEM((FFN_ROWS, d_ff), jnp.bfloat16)]
        + _weight_scratch(w_in_all.shape[2:]) + _weight_scratch(w_out_all.shape[2:]),
        compiler_params=pltpu.CompilerParams(
            dimension_semantics=("arbitrary",), vmem_limit_bytes=VMEM_LIMIT_BYTES),
        name="ffn",
    )(x2d, g_in, g_out, w_in_all, w_out_all)


def _conv_kernel(x_ref, g_pre_ref, g_post_ref, w_in_hbm, b_in_ref, w_dw_ref, b_dw_ref,
                 ln_g_ref, ln_b_ref, w_out_hbm, b_out_ref, o_ref,
                 hext_ref, shift_ref, taps_ref, conv_ref,
                 w_in_ref, stage_in_ref, sem_in, w_out_ref, stage_out_ref, sem_out, *, idx):
    d = x_ref.shape[-1]
    rows = x_ref.shape[1]
    n_ext = CONV_HALO + rows
    first = pl.program_id(1) == 0

    @pl.when(jnp.logical_and(pl.program_id(0) == 0, first))
    def _():
        _load_as_bf16(w_in_hbm.at[idx], w_in_ref, stage_in_ref, sem_in)
        _load_as_bf16(w_out_hbm.at[idx], w_out_ref, stage_out_ref, sem_out)

    @pl.when(first)
    def _():
        hext_ref[0:CONV_HALO, :] = jnp.zeros((CONV_HALO, d), jnp.float32)
        hext_ref[n_ext:n_ext + V7X_SUBLANES, :] = jnp.zeros((V7X_SUBLANES, d), jnp.float32)

    @pl.when(jnp.logical_not(first))
    def _():
        hext_ref[0:CONV_HALO, :] = hext_ref[rows:n_ext, :]

    x = x_ref[0]
    xn = _rms(x, g_pre_ref[...]).astype(jnp.bfloat16)
    for j in range(CONV_WIDTH):
        taps_ref[j] = jnp.broadcast_to(w_dw_ref[j:j + 1, :], (V7X_SUBLANES, d))

    sub = lax.broadcasted_iota(jnp.int32, (V7X_SUBLANES, CONV_LANES), 0)
    n_tiles = n_ext // V7X_SUBLANES
    for c0 in range(0, d, CONV_LANES):
        lanes = slice(c0, c0 + CONV_LANES)
        gate_lanes = slice(d + c0, d + c0 + CONV_LANES)
        a = _bdot(xn, w_in_ref[:, lanes]) + b_in_ref[:, lanes]
        gate = _bdot(xn, w_in_ref[:, gate_lanes]) + b_in_ref[:, gate_lanes]
        hext_ref[CONV_HALO:n_ext, lanes] = a * _sigmoid(gate)

        for s in range(1, V7X_SUBLANES):
            rot = [pltpu.roll(hext_ref[i * V7X_SUBLANES:(i + 1) * V7X_SUBLANES, lanes],
                              V7X_SUBLANES - s, axis=0) for i in range(n_tiles + 1)]
            for i in range(n_tiles):
                shift_ref[s - 1, i * V7X_SUBLANES:(i + 1) * V7X_SUBLANES, lanes] = jnp.where(
                    sub < V7X_SUBLANES - s, rot[i], rot[i + 1])

        for r0 in range(0, rows, CONV_SUB):
            acc = [jnp.zeros((V7X_SUBLANES, CONV_LANES), jnp.float32)
                   for _ in range(CONV_SUB // V7X_SUBLANES)]
            for j in range(CONV_WIDTH):
                off = CONV_HALO - (CONV_WIDTH - 1) + j
                lo = r0 + off - off % V7X_SUBLANES
                tap = taps_ref[j, :, c0:c0 + CONV_LANES]
                for i in range(len(acc)):
                    r = lo + i * V7X_SUBLANES
                    if off % V7X_SUBLANES == 0:
                        win = hext_ref[r:r + V7X_SUBLANES, c0:c0 + CONV_LANES]
                    else:
                        win = shift_ref[off % V7X_SUBLANES - 1, r:r + V7X_SUBLANES,
                                        c0:c0 + CONV_LANES]
                    acc[i] = acc[i] + tap * win
            for i in range(len(acc)):
                r = r0 + i * V7X_SUBLANES
                conv_ref[r:r + V7X_SUBLANES, c0:c0 + CONV_LANES] = acc[i]

    h = conv_ref[...] + b_dw_ref[...]
    mu = jnp.mean(h, axis=-1, keepdims=True)
    hc = h - mu
    hn = hc * lax.rsqrt(jnp.mean(hc * hc, axis=-1, keepdims=True) + EPS)
    hn = hn * ln_g_ref[...] + ln_b_ref[...]
    act = (hn * _sigmoid(hn)).astype(jnp.bfloat16)
    m = _bdot(act, w_out_ref[...]) + b_out_ref[...]
    o_ref[0] = x + _rms(m, g_post_ref[...])


def _conv_mixer(x, g_pre, g_post, w_in_all, b_in, w_dw, b_dw, ln_g, ln_b, w_out_all, b_out, *, idx):
    bsz, seq, d = x.shape
    assert seq % CONV_ROWS == 0 and CONV_ROWS % CONV_SUB == 0
    assert CONV_HALO >= CONV_WIDTH - 1 and CONV_HALO % V7X_SUBLANES == 0
    n_ext = CONV_HALO + CONV_ROWS
    row = pl.BlockSpec((1, CONV_ROWS, d), lambda b, s: (b, s, 0))
    vec = _resident((1, d))
    hbm = pl.BlockSpec(memory_space=pl.ANY)
    return pl.pallas_call(
        functools.partial(_conv_kernel, idx=idx),
        grid=(bsz, seq // CONV_ROWS),
        in_specs=[row, vec, vec, hbm, _resident((1, 2 * d)),
                  _resident(w_dw.shape), vec, vec, vec, hbm, vec],
        out_specs=row,
        out_shape=jax.ShapeDtypeStruct(x.shape, jnp.float32),
        scratch_shapes=[pltpu.VMEM((n_ext + V7X_SUBLANES, d), jnp.float32),
                        pltpu.VMEM((V7X_SUBLANES - 1, n_ext, d), jnp.float32),
                        pltpu.VMEM((CONV_WIDTH, V7X_SUBLANES, d), jnp.float32),
                        pltpu.VMEM((CONV_ROWS, d), jnp.float32)]
        + _weight_scratch(w_in_all.shape[1:]) + _weight_scratch(w_out_all.shape[1:]),
        compiler_params=pltpu.CompilerParams(
            dimension_semantics=("arbitrary", "arbitrary"), vmem_limit_bytes=VMEM_LIMIT_BYTES),
        name="conv_mixer",
    )(x, g_pre, g_post, w_in_all, b_in, w_dw, b_dw, ln_g, ln_b, w_out_all, b_out)


HGRN_LEVELS = tuple(HGRN_CHUNK >> (i + 1) for i in range(HGRN_CHUNK.bit_length() - 1))
HGRN_MXU_LEVELS = tuple(h for h in HGRN_LEVELS if 2 * h < V7X_SUBLANES)
LOG2_E = 1.4426950408889634


def _hgrn_sum_matrix():
    c = HGRN_CHUNK
    t = np.arange(c)[:, None]
    u = np.arange(c)[None, :]
    blocks = [u <= t]
    for h in HGRN_MXU_LEVELS:
        mid = (t // (2 * h)) * (2 * h) + h
        upper = t >= mid
        blocks.append(np.where(upper, (u >= mid) & (u <= t), (u > t) & (u < mid)))
    return np.concatenate(blocks, axis=0).astype(np.float32)


def _split3(x):
    hi = x.astype(jnp.bfloat16)
    r1 = x - hi.astype(jnp.float32)
    mid = r1.astype(jnp.bfloat16)
    lo = (r1 - mid.astype(jnp.float32)).astype(jnp.bfloat16)
    return hi, mid, lo


def _dot_nt(a, b):
    return lax.dot_general(a, b, (((1,), (1,)), ((), ())), preferred_element_type=jnp.float32)


def _dot_tn(a, b):
    return lax.dot_general(a, b, (((0,), (0,)), ((), ())), preferred_element_type=jnp.float32)


def _hgrn_kernel(x_ref, g_pre_ref, g_post_ref, w_in_hbm, lb_logits_ref, g_norm_ref, w_out_hbm,
                 sum_ref, o_ref, state_ref,
                 w_in_ref, stage_in_ref, sem_in, w_out_ref, stage_out_ref, sem_out, *, layer, idx):
    d = x_ref.shape[-1]
    rows = x_ref.shape[1]
    c = HGRN_CHUNK
    sub = HGRN_SUB
    n_sub = rows // sub
    chunks_per_sub = sub // c
    grp = HGRN_GROUP * c
    heads = [slice(hd * HEAD_DIM, (hd + 1) * HEAD_DIM) for hd in range(HGRN_HEADS)]

    @pl.when(jnp.logical_and(pl.program_id(0) == 0, pl.program_id(1) == 0))
    def _():
        _load_as_bf16(w_in_hbm.at[idx], w_in_ref, stage_in_ref, sem_in)
        _load_as_bf16(w_out_hbm.at[idx], w_out_ref, stage_out_ref, sem_out)

    @pl.when(pl.program_id(1) == 0)
    def _():
        state_ref[...] = jnp.zeros(state_ref.shape, jnp.float32)

    logits = lb_logits_ref[...]
    e = jnp.exp(logits - jnp.max(logits, axis=0, keepdims=True))
    p = e / jnp.sum(e, axis=0, keepdims=True)
    cum = p[0:1]
    for l in range(1, layer + 1):
        cum = cum + p[l:l + 1]
    lb = cum - p[0:1]

    tt = lax.broadcasted_iota(jnp.int32, (grp, grp), 0)
    ss = lax.broadcasted_iota(jnp.int32, (grp, grp), 1)
    diag_mask = tt == ss
    masks = []
    for h in HGRN_LEVELS:
        shift = (2 * h).bit_length() - 1
        masks.append(((tt >> shift) == (ss >> shift)) & ((tt & h) != 0) & ((ss & h) == 0))

    def project(si):
        x = x_ref[0, si * sub:(si + 1) * sub, :]
        xn = _rms(x, g_pre_ref[...]).astype(jnp.bfloat16)
        f = _bdot(xn, w_in_ref[:, d:2 * d])
        q = _bdot(xn, w_in_ref[:, 0:d])
        v = _bdot(xn, w_in_ref[:, 2 * d:3 * d]).astype(jnp.bfloat16)
        gate = _bdot(xn, w_in_ref[:, 3 * d:4 * d])
        q = q * _sigmoid(q)
        t = jnp.exp(-jnp.abs(f))
        w = 1.0 + t
        r = 1.0 / w
        a1 = jnp.log(lb)
        a2 = jnp.log1p(-lb) + (jnp.minimum(f, 0.0) - jnp.log(w))
        log_f = jnp.maximum(a1, a2) + jnp.log(1.0 + jnp.exp(-jnp.abs(a1 - a2)))
        k = (1.0 - lb) * jnp.where(f >= 0.0, t * r, r)
        return dict(x=x, q=q, k=k, v=v, lf2=log_f * LOG2_E, gate=gate)

    def exponents(t, ci):
        hi, mid, lo = _split3(t["lf2"][ci * c:(ci + 1) * c])
        return _bdot(sum_ref[...], jnp.concatenate([hi, mid, lo], axis=0))

    def chunk_factors(t, ci, ex):
        qc = t["q"][ci * c:(ci + 1) * c]
        kc = t["k"][ci * c:(ci + 1) * c]
        b = ex[0:c]
        b_last = b[c - 1:c]
        q_dec = (qc * jnp.exp2(b)).astype(jnp.bfloat16)
        k_dec = (kc * jnp.exp2(b_last - b)).astype(jnp.bfloat16)
        q_bf, k_bf = qc.astype(jnp.bfloat16), kc.astype(jnp.bfloat16)
        q_lv, k_lv = [], []
        mxu_block = 1
        for h in HGRN_LEVELS:
            if h in HGRN_MXU_LEVELS:
                expo = ex[mxu_block * c:(mxu_block + 1) * c]
                mxu_block += 1
            else:
                pieces = []
                for g in range(0, c, 2 * h):
                    ref = b[g + h - 1:g + h]
                    if h >= V7X_SUBLANES:
                        pieces += [ref - b[g:g + h], b[g + h:g + 2 * h] - ref]
                    else:
                        pieces.append(-jnp.abs(b[g:g + 2 * h] - ref))
                expo = jnp.concatenate(pieces, axis=0)
            fac = jnp.exp2(expo).astype(jnp.bfloat16)
            q_lv.append(q_bf * fac)
            k_lv.append(k_bf * fac)
        return dict(q_lv=q_lv, k_lv=k_lv, qk=qc * kc, q_dec=q_dec, k_dec=k_dec,
                    decay_last=jnp.exp2(b_last))

    def group_scores(chunks):
        rows_of = lambda key, l=None: jnp.concatenate(
            [ch[key] if l is None else ch[key][l] for ch in chunks], axis=0)
        qk = rows_of("qk")
        q_lv = [rows_of("q_lv", l) for l in range(len(HGRN_LEVELS))]
        k_lv = [rows_of("k_lv", l) for l in range(len(HGRN_LEVELS))]
        scores = []
        for sl in heads:
            acc = jnp.where(diag_mask, jnp.sum(qk[:, sl], axis=-1, keepdims=True), 0.0)
            for ql, kl, mask in zip(q_lv, k_lv, masks):
                acc = jnp.where(mask, _dot_nt(ql[:, sl], kl[:, sl]), acc)
            scores.append(acc.astype(jnp.bfloat16))
        return scores

    def state_path(t, ci, ch):
        vc = t["v"][ci * c:(ci + 1) * c]
        outs = []
        for hd, sl in enumerate(heads):
            state_t = state_ref[hd]
            outs.append(_dot_nt(ch["q_dec"][:, sl], state_t.astype(jnp.bfloat16)))
            state_ref[hd] = (state_t * ch["decay_last"][:, sl]
                             + _dot_tn(vc[:, sl], ch["k_dec"][:, sl]))
        return outs

    def finish(si, t, o):
        gate = t["gate"] * _sigmoid(t["gate"])
        parts = []
        for sl in heads:
            oh = o[:, sl]
            parts.append(oh * lax.rsqrt(jnp.mean(oh * oh, axis=-1, keepdims=True) + EPS))
        on = jnp.concatenate(parts, axis=-1) * g_norm_ref[...] * gate
        m = _bdot(on.astype(jnp.bfloat16), w_out_ref[...])
        o_ref[0, si * sub:(si + 1) * sub, :] = t["x"] + _rms(m, g_post_ref[...])

    tiles = [project(si) for si in range(n_sub)]
    pending = None
    for si, t in enumerate(tiles):
        exs = [exponents(t, ci) for ci in range(chunks_per_sub)]
        if pending is not None:
            finish(*pending)
        chunks = [chunk_factors(t, ci, exs[ci]) for ci in range(chunks_per_sub)]
        scores = [group_scores(chunks[g:g + HGRN_GROUP])
                  for g in range(0, chunks_per_sub, HGRN_GROUP)]
        o_inter = [state_path(t, ci, ch) for ci, ch in enumerate(chunks)]
        o_rows = []
        for gi, sc in enumerate(scores):
            vg = t["v"][gi * grp:(gi + 1) * grp]
            inter = [jnp.concatenate([o_inter[gi * HGRN_GROUP + j][hd] for j in range(HGRN_GROUP)],
                                     axis=0) for hd in range(HGRN_HEADS)]
            o_rows.append(jnp.concatenate(
                [inter[hd] + _bdot(sc[hd], vg[:, sl]) for hd, sl in enumerate(heads)], axis=-1))
        pending = (si, t, jnp.concatenate(o_rows, axis=0))
    finish(*pending)


def _hgrn_mixer(x, g_pre, g_post, w_in_all, lb_logits, g_norm, w_out_all, *, layer, idx):
    bsz, seq, d = x.shape
    assert seq % HGRN_ROWS == 0 and HGRN_ROWS % HGRN_SUB == 0
    assert HGRN_SUB % (HGRN_GROUP * HGRN_CHUNK) == 0
    assert d == HGRN_HEADS * HEAD_DIM and w_in_all.shape[1:] == (d, 4 * d)
    sum_mat = _hgrn_sum_matrix()
    sum3 = jnp.asarray(np.concatenate([sum_mat] * 3, axis=1), jnp.bfloat16)
    g_norm_row = jnp.tile(g_norm.reshape(1, HEAD_DIM), (1, HGRN_HEADS))
    row = pl.BlockSpec((1, HGRN_ROWS, d), lambda b, s: (b, s, 0))
    vec = _resident((1, d))
    hbm = pl.BlockSpec(memory_space=pl.ANY)
    return pl.pallas_call(
        functools.partial(_hgrn_kernel, layer=layer, idx=idx),
        grid=(bsz, seq // HGRN_ROWS),
        in_specs=[row, vec, vec, hbm, _resident(lb_logits.shape), vec, hbm,
                  _resident(sum3.shape)],
        out_specs=row,
        out_shape=jax.ShapeDtypeStruct(x.shape, jnp.float32),
        scratch_shapes=[pltpu.VMEM((HGRN_HEADS, HEAD_DIM, HEAD_DIM), jnp.float32)]
        + _weight_scratch(w_in_all.shape[1:]) + _weight_scratch(w_out_all.shape[1:]),
        compiler_params=pltpu.CompilerParams(
            dimension_semantics=("arbitrary", "arbitrary"), vmem_limit_bytes=VMEM_LIMIT_BYTES),
        name="hgrn_mixer",
    )(x, g_pre, g_post, w_in_all, lb_logits, g_norm_row, w_out_all, sum3)


def kernel(x, norm_gains, ffn_w_in, ffn_w_out, conv_w_in, conv_b_in, conv_w_dw, conv_b_dw,
           conv_ln_g, conv_ln_b, conv_w_out, conv_b_out, hgrn_w_in, hgrn_lb_logits,
           hgrn_g_norm, hgrn_w_out):
    bsz, seq, d = x.shape
    depth = norm_gains.shape[0]
    vec = lambda v: v.reshape(1, -1).astype(jnp.float32)

    def ffn(x, layer, which):
        g = norm_gains[layer]
        y = _ffn(x.reshape(bsz * seq, d), vec(g[4 * which]), vec(g[4 * which + 1]),
                 ffn_w_in, ffn_w_out, layer=layer, which=which)
        return y.reshape(bsz, seq, d)

    for layer in range(depth):
        g = norm_gains[layer]
        x = ffn(x, layer, 0)
        j = layer // 2
        if layer % 2 == 0:
            x = _conv_mixer(x, vec(g[2]), vec(g[3]), conv_w_in, vec(conv_b_in[j]),
                            conv_w_dw[j], vec(conv_b_dw[j]), vec(conv_ln_g[j]),
                            vec(conv_ln_b[j]), conv_w_out, vec(conv_b_out[j]), idx=j)
        else:
            x = _hgrn_mixer(x, vec(g[2]), vec(g[3]), hgrn_w_in, hgrn_lb_logits,
                            hgrn_g_norm[j], hgrn_w_out, layer=layer, idx=j)
        x = ffn(x, layer, 1)
    return x
```

```python
import functools

import jax
import jax.numpy as jnp
import numpy as np
from jax import lax
from jax.experimental import pallas as pl
from jax.experimental.pallas import tpu as pltpu

EPS = 1e-6
CONV_WIDTH = 31
HGRN_HEADS = 8
HEAD_DIM = 128

V7X_SUBLANES = 8
V7X_VMEM_BYTES = 64 * 1024 * 1024
VMEM_LIMIT_BYTES = 56 * 1024 * 1024

FFN_ROWS = 1024
FFN_COLS = 256
FFN_OUT_ROWS = 512
WEIGHT_STAGE_SLOTS = 8
WEIGHT_STAGE_BYTES = 768 * 1024
BF16_ROWS_PER_VREG = 2 * V7X_SUBLANES
CONV_ROWS = 512
CONV_HALO = 32
CONV_SUB = 32
CONV_LANES = 256
HGRN_ROWS = 512
HGRN_SUB = 512
HGRN_CHUNK = 64
HGRN_GROUP = 2


def _rms(x, gain):
    return x * lax.rsqrt(jnp.mean(x * x, axis=-1, keepdims=True) + EPS) * gain


def _sigmoid(x):
    return 1.0 / (1.0 + jnp.exp(-x))


def _bdot(a, b):
    return jnp.dot(a, b, preferred_element_type=jnp.float32)


def _resident(shape):
    return pl.BlockSpec(shape, lambda *_: (0,) * len(shape), pipeline_mode=pl.Buffered(1))


def _weight_scratch(shape):
    k, n = shape
    cap = WEIGHT_STAGE_BYTES // (4 * n)
    stage_rows = max(r for r in range(BF16_ROWS_PER_VREG, cap + 1, BF16_ROWS_PER_VREG)
                     if k % r == 0)
    return [pltpu.VMEM((k, n), jnp.bfloat16),
            pltpu.VMEM((WEIGHT_STAGE_SLOTS, stage_rows, n), jnp.float32),
            pltpu.SemaphoreType.DMA((WEIGHT_STAGE_SLOTS,))]


def _load_as_bf16(src_hbm, dst_ref, stage_ref, sem):
    chunk = stage_ref.shape[1]
    n = src_hbm.shape[0] // chunk
    assert n * chunk == src_hbm.shape[0] and stage_ref.shape[0] == WEIGHT_STAGE_SLOTS

    def copy(c):
        slot = c % WEIGHT_STAGE_SLOTS
        return pltpu.make_async_copy(src_hbm.at[pl.ds(c * chunk, chunk), :],
                                     stage_ref.at[slot], sem.at[slot])

    ahead = WEIGHT_STAGE_SLOTS - 1
    for c in range(min(ahead, n)):
        copy(c).start()
    for c in range(n):
        if c + ahead < n:
            copy(c + ahead).start()
        copy(c).wait()
        dst_ref[c * chunk:(c + 1) * chunk, :] = stage_ref[c % WEIGHT_STAGE_SLOTS].astype(
            jnp.bfloat16)


def _ffn_kernel(x_ref, g_in_ref, g_out_ref, w_in_hbm, w_out_hbm, o_ref, act_ref,
                w_in_ref, stage_in_ref, sem_in, w_out_ref, stage_out_ref, sem_out,
                *, d_ff, layer, which):
    @pl.when(pl.program_id(0) == 0)
    def _():
        _load_as_bf16(w_in_hbm.at[layer, which], w_in_ref, stage_in_ref, sem_in)
        _load_as_bf16(w_out_hbm.at[layer, which], w_out_ref, stage_out_ref, sem_out)

    x = x_ref[...]
    xn = _rms(x, g_in_ref[...]).astype(jnp.bfloat16)
    for c0 in range(0, d_ff, FFN_COLS):
        gate = _bdot(xn, w_in_ref[:, c0:c0 + FFN_COLS])
        up = _bdot(xn, w_in_ref[:, d_ff + c0:d_ff + c0 + FFN_COLS])
        act_ref[:, c0:c0 + FFN_COLS] = (gate * _sigmoid(gate) * up).astype(jnp.bfloat16)
    for r0 in range(0, x.shape[0], FFN_OUT_ROWS):
        rs = slice(r0, r0 + FFN_OUT_ROWS)
        h = _bdot(act_ref[rs, :], w_out_ref[...])
        o_ref[rs, :] = x[rs] + 0.5 * _rms(h, g_out_ref[...])


def _ffn(x2d, g_in, g_out, w_in_all, w_out_all, *, layer, which):
    m, d = x2d.shape
    d_ff = w_out_all.shape[2]
    assert m % FFN_ROWS == 0 and d_ff % FFN_COLS == 0
    row = pl.BlockSpec((FFN_ROWS, d), lambda i: (i, 0))
    hbm = pl.BlockSpec(memory_space=pl.ANY)
    return pl.pallas_call(
        functools.partial(_ffn_kernel, d_ff=d_ff, layer=layer, which=which),
        grid=(m // FFN_ROWS,),
        in_specs=[row, _resident((1, d)), _resident((1, d)), hbm, hbm],
        out_specs=row,
        out_shape=jax.ShapeDtypeStruct((m, d), jnp.float32),
        scratch_shapes=[pltpu.VMEM((FFN_ROWS, d_ff), jnp.bfloat16)]
        + _weight_scratch(w_in_all.shape[2:]) + _weight_scratch(w_out_all.shape[2:]),
        compiler_params=pltpu.CompilerParams(
            dimension_semantics=("arbitrary",), vmem_limit_bytes=VMEM_LIMIT_BYTES),
        name="ffn",
    )(x2d, g_in, g_out, w_in_all, w_out_all)


def _conv_kernel(x_ref, g_pre_ref, g_post_ref, w_in_hbm, b_in_ref, w_dw_ref, b_dw_ref,
                 ln_g_ref, ln_b_ref, w_out_hbm, b_out_ref, o_ref,
                 hext_ref, shift_ref, taps_ref, conv_ref,
                 w_in_ref, stage_in_ref, sem_in, w_out_ref, stage_out_ref, sem_out, *, idx):
    d = x_ref.shape[-1]
    rows = x_ref.shape[1]
    n_ext = CONV_HALO + rows
    first = pl.program_id(1) == 0

    @pl.when(jnp.logical_and(pl.program_id(0) == 0, first))
    def _():
        _load_as_bf16(w_in_hbm.at[idx], w_in_ref, stage_in_ref, sem_in)
        _load_as_bf16(w_out_hbm.at[idx], w_out_ref, stage_out_ref, sem_out)

    @pl.when(first)
    def _():
        hext_ref[0:CONV_HALO, :] = jnp.zeros((CONV_HALO, d), jnp.float32)
        hext_ref[n_ext:n_ext + V7X_SUBLANES, :] = jnp.zeros((V7X_SUBLANES, d), jnp.float32)

    @pl.when(jnp.logical_not(first))
    def _():
        hext_ref[0:CONV_HALO, :] = hext_ref[rows:n_ext, :]

    x = x_ref[0]
    xn = _rms(x, g_pre_ref[...]).astype(jnp.bfloat16)
    for j in range(CONV_WIDTH):
        taps_ref[j] = jnp.broadcast_to(w_dw_ref[j:j + 1, :], (V7X_SUBLANES, d))

    sub = lax.broadcasted_iota(jnp.int32, (V7X_SUBLANES, CONV_LANES), 0)
    n_tiles = n_ext // V7X_SUBLANES
    for c0 in range(0, d, CONV_LANES):
        lanes = slice(c0, c0 + CONV_LANES)
        gate_lanes = slice(d + c0, d + c0 + CONV_LANES)
        a = _bdot(xn, w_in_ref[:, lanes]) + b_in_ref[:, lanes]
        gate = _bdot(xn, w_in_ref[:, gate_lanes]) + b_in_ref[:, gate_lanes]
        hext_ref[CONV_HALO:n_ext, lanes] = a * _sigmoid(gate)

        for s in range(1, V7X_SUBLANES):
            rot = [pltpu.roll(hext_ref[i * V7X_SUBLANES:(i + 1) * V7X_SUBLANES, lanes],
                              V7X_SUBLANES - s, axis=0) for i in range(n_tiles + 1)]
            for i in range(n_tiles):
                shift_ref[s - 1, i * V7X_SUBLANES:(i + 1) * V7X_SUBLANES, lanes] = jnp.where(
                    sub < V7X_SUBLANES - s, rot[i], rot[i + 1])

        for r0 in range(0, rows, CONV_SUB):
            acc = [jnp.zeros((V7X_SUBLANES, CONV_LANES), jnp.float32)
                   for _ in range(CONV_SUB // V7X_SUBLANES)]
            for j in range(CONV_WIDTH):
                off = CONV_HALO - (CONV_WIDTH - 1) + j
                lo = r0 + off - off % V7X_SUBLANES
                tap = taps_ref[j, :, c0:c0 + CONV_LANES]
                for i in range(len(acc)):
                    r = lo + i * V7X_SUBLANES
                    if off % V7X_SUBLANES == 0:
                        win = hext_ref[r:r + V7X_SUBLANES, c0:c0 + CONV_LANES]
                    else:
                        win = shift_ref[off % V7X_SUBLANES - 1, r:r + V7X_SUBLANES,
                                        c0:c0 + CONV_LANES]
                    acc[i] = acc[i] + tap * win
            for i in range(len(acc)):
                r = r0 + i * V7X_SUBLANES
                conv_ref[r:r + V7X_SUBLANES, c0:c0 + CONV_LANES] = acc[i]

    h = conv_ref[...] + b_dw_ref[...]
    mu = jnp.mean(h, axis=-1, keepdims=True)
    hc = h - mu
    hn = hc * lax.rsqrt(jnp.mean(hc * hc, axis=-1, keepdims=True) + EPS)
    hn = hn * ln_g_ref[...] + ln_b_ref[...]
    act = (hn * _sigmoid(hn)).astype(jnp.bfloat16)
    m = _bdot(act, w_out_ref[...]) + b_out_ref[...]
    o_ref[0] = x + _rms(m, g_post_ref[...])


def _conv_mixer(x, g_pre, g_post, w_in_all, b_in, w_dw, b_dw, ln_g, ln_b, w_out_all, b_out, *, idx):
    bsz, seq, d = x.shape
    assert seq % CONV_ROWS == 0 and CONV_ROWS % CONV_SUB == 0
    assert CONV_HALO >= CONV_WIDTH - 1 and CONV_HALO % V7X_SUBLANES == 0
    n_ext = CONV_HALO + CONV_ROWS
    row = pl.BlockSpec((1, CONV_ROWS, d), lambda b, s: (b, s, 0))
    vec = _resident((1, d))
    hbm = pl.BlockSpec(memory_space=pl.ANY)
    return pl.pallas_call(
        functools.partial(_conv_kernel, idx=idx),
        grid=(bsz, seq // CONV_ROWS),
        in_specs=[row, vec, vec, hbm, _resident((1, 2 * d)),
                  _resident(w_dw.shape), vec, vec, vec, hbm, vec],
        out_specs=row,
        out_shape=jax.ShapeDtypeStruct(x.shape, jnp.float32),
        scratch_shapes=[pltpu.VMEM((n_ext + V7X_SUBLANES, d), jnp.float32),
                        pltpu.VMEM((V7X_SUBLANES - 1, n_ext, d), jnp.float32),
                        pltpu.VMEM((CONV_WIDTH, V7X_SUBLANES, d), jnp.float32),
                        pltpu.VMEM((CONV_ROWS, d), jnp.float32)]
        + _weight_scratch(w_in_all.shape[1:]) + _weight_scratch(w_out_all.shape[1:]),
        compiler_params=pltpu.CompilerParams(
            dimension_semantics=("arbitrary", "arbitrary"), vmem_limit_bytes=VMEM_LIMIT_BYTES),
        name="conv_mixer",
    )(x, g_pre, g_post, w_in_all, b_in, w_dw, b_dw, ln_g, ln_b, w_out_all, b_out)


HGRN_LEVELS = tuple(HGRN_CHUNK >> (i + 1) for i in range(HGRN_CHUNK.bit_length() - 1))
HGRN_MXU_LEVELS = tuple(h for h in HGRN_LEVELS if 2 * h < V7X_SUBLANES)
LOG2_E = 1.4426950408889634


def _hgrn_sum_matrix():
    c = HGRN_CHUNK
    t = np.arange(c)[:, None]
    u = np.arange(c)[None, :]
    blocks = [u <= t]
    for h in HGRN_MXU_LEVELS:
        mid = (t // (2 * h)) * (2 * h) + h
        upper = t >= mid
        blocks.append(np.where(upper, (u >= mid) & (u <= t), (u > t) & (u < mid)))
    return np.concatenate(blocks, axis=0).astype(np.float32)


def _split3(x):
    hi = x.astype(jnp.bfloat16)
    r1 = x - hi.astype(jnp.float32)
    mid = r1.astype(jnp.bfloat16)
    lo = (r1 - mid.astype(jnp.float32)).astype(jnp.bfloat16)
    return hi, mid, lo


def _dot_nt(a, b):
    return lax.dot_general(a, b, (((1,), (1,)), ((), ())), preferred_element_type=jnp.float32)


def _dot_tn(a, b):
    return lax.dot_general(a, b, (((0,), (0,)), ((), ())), preferred_element_type=jnp.float32)


def _hgrn_kernel(x_ref, g_pre_ref, g_post_ref, w_in_hbm, lb_logits_ref, g_norm_ref, w_out_hbm,
                 sum_ref, o_ref, state_ref,
                 w_in_ref, stage_in_ref, sem_in, w_out_ref, stage_out_ref, sem_out, *, layer, idx):
    d = x_ref.shape[-1]
    rows = x_ref.shape[1]
    c = HGRN_CHUNK
    sub = HGRN_SUB
    n_sub = rows // sub
    chunks_per_sub = sub // c
    grp = HGRN_GROUP * c
    heads = [slice(hd * HEAD_DIM, (hd + 1) * HEAD_DIM) for hd in range(HGRN_HEADS)]

    @pl.when(jnp.logical_and(pl.program_id(0) == 0, pl.program_id(1) == 0))
    def _():
        _load_as_bf16(w_in_hbm.at[idx], w_in_ref, stage_in_ref, sem_in)
        _load_as_bf16(w_out_hbm.at[idx], w_out_ref, stage_out_ref, sem_out)

    @pl.when(pl.program_id(1) == 0)
    def _():
        state_ref[...] = jnp.zeros(state_ref.shape, jnp.float32)

    logits = lb_logits_ref[...]
    e = jnp.exp(logits - jnp.max(logits, axis=0, keepdims=True))
    p = e / jnp.sum(e, axis=0, keepdims=True)
    cum = p[0:1]
    for l in range(1, layer + 1):
        cum = cum + p[l:l + 1]
    lb = cum - p[0:1]

    tt = lax.broadcasted_iota(jnp.int32, (grp, grp), 0)
    ss = lax.broadcasted_iota(jnp.int32, (grp, grp), 1)
    diag_mask = tt == ss
    masks = []
    for h in HGRN_LEVELS:
        shift = (2 * h).bit_length() - 1
        masks.append(((tt >> shift) == (ss >> shift)) & ((tt & h) != 0) & ((ss & h) == 0))

    def project(si):
        x = x_ref[0, si * sub:(si + 1) * sub, :]
        xn = _rms(x, g_pre_ref[...]).astype(jnp.bfloat16)
        f = _bdot(xn, w_in_ref[:, d:2 * d])
        q = _bdot(xn, w_in_ref[:, 0:d])
        v = _bdot(xn, w_in_ref[:, 2 * d:3 * d]).astype(jnp.bfloat16)
        gate = _bdot(xn, w_in_ref[:, 3 * d:4 * d])
        q = q * _sigmoid(q)
        t = jnp.exp(-jnp.abs(f))
        w = 1.0 + t
        r = 1.0 / w
        a1 = jnp.log(lb)
        a2 = jnp.log1p(-lb) + (jnp.minimum(f, 0.0) - jnp.log(w))
        log_f = jnp.maximum(a1, a2) + jnp.log(1.0 + jnp.exp(-jnp.abs(a1 - a2)))
        k = (1.0 - lb) * jnp.where(f >= 0.0, t * r, r)
        return dict(x=x, q=q, k=k, v=v, lf2=log_f * LOG2_E, gate=gate)

    def exponents(t, ci):
        hi, mid, lo = _split3(t["lf2"][ci * c:(ci + 1) * c])
        return _bdot(sum_ref[...], jnp.concatenate([hi, mid, lo], axis=0))

    def chunk_factors(t, ci, ex):
        qc = t["q"][ci * c:(ci + 1) * c]
        kc = t["k"][ci * c:(ci + 1) * c]
        b = ex[0:c]
        b_last = b[c - 1:c]
        q_dec = (qc * jnp.exp2(b)).astype(jnp.bfloat16)
        k_dec = (kc * jnp.exp2(b_last - b)).astype(jnp.bfloat16)
        q_bf, k_bf = qc.astype(jnp.bfloat16), kc.astype(jnp.bfloat16)
        q_lv, k_lv = [], []
        mxu_block = 1
        for h in HGRN_LEVELS:
            if h in HGRN_MXU_LEVELS:
                expo = ex[mxu_block * c:(mxu_block + 1) * c]
                mxu_block += 1
            else:
                pieces = []
                for g in range(0, c, 2 * h):
                    ref = b[g + h - 1:g + h]
                    if h >= V7X_SUBLANES:
                        pieces += [ref - b[g:g + h], b[g + h:g + 2 * h] - ref]
                    else:
                        pieces.append(-jnp.abs(b[g:g + 2 * h] - ref))
                expo = jnp.concatenate(pieces, axis=0)
            fac = jnp.exp2(expo).astype(jnp.bfloat16)
            q_lv.append(q_bf * fac)
            k_lv.append(k_bf * fac)
        return dict(q_lv=q_lv, k_lv=k_lv, qk=qc * kc, q_dec=q_dec, k_dec=k_dec,
                    decay_last=jnp.exp2(b_last))

    def group_scores(chunks):
        rows_of = lambda key, l=None: jnp.concatenate(
            [ch[key] if l is None else ch[key][l] for ch in chunks], axis=0)
        qk = rows_of("qk")
        q_lv = [rows_of("q_lv", l) for l in range(len(HGRN_LEVELS))]
        k_lv = [rows_of("k_lv", l) for l in range(len(HGRN_LEVELS))]
        scores = []
        for sl in heads:
            acc = jnp.where(diag_mask, jnp.sum(qk[:, sl], axis=-1, keepdims=True), 0.0)
            for ql, kl, mask in zip(q_lv, k_lv, masks):
                acc = jnp.where(mask, _dot_nt(ql[:, sl], kl[:, sl]), acc)
            scores.append(acc.astype(jnp.bfloat16))
        return scores

    def state_path(t, ci, ch):
        vc = t["v"][ci * c:(ci + 1) * c]
        outs = []
        for hd, sl in enumerate(heads):
            state_t = state_ref[hd]
            outs.append(_dot_nt(ch["q_dec"][:, sl], state_t.astype(jnp.bfloat16)))
            state_ref[hd] = (state_t * ch["decay_last"][:, sl]
                             + _dot_tn(vc[:, sl], ch["k_dec"][:, sl]))
        return outs

    def finish(si, t, o):
        gate = t["gate"] * _sigmoid(t["gate"])
        parts = []
        for sl in heads:
            oh = o[:, sl]
            parts.append(oh * lax.rsqrt(jnp.mean(oh * oh, axis=-1, keepdims=True) + EPS))
        on = jnp.concatenate(parts, axis=-1) * g_norm_ref[...] * gate
        m = _bdot(on.astype(jnp.bfloat16), w_out_ref[...])
        o_ref[0, si * sub:(si + 1) * sub, :] = t["x"] + _rms(m, g_post_ref[...])

    tiles = [project(si) for si in range(n_sub)]
    pending = None
    for si, t in enumerate(tiles):
        exs = [exponents(t, ci) for ci in range(chunks_per_sub)]
        if pending is not None:
            finish(*pending)
        chunks = [chunk_factors(t, ci, exs[ci]) for ci in range(chunks_per_sub)]
        scores = [group_scores(chunks[g:g + HGRN_GROUP])
                  for g in range(0, chunks_per_sub, HGRN_GROUP)]
        o_inter = [state_path(t, ci, ch) for ci, ch in enumerate(chunks)]
        o_rows = []
        for gi, sc in enumerate(scores):
            vg = t["v"][gi * grp:(gi + 1) * grp]
            inter = [jnp.concatenate([o_inter[gi * HGRN_GROUP + j][hd] for j in range(HGRN_GROUP)],
                                     axis=0) for hd in range(HGRN_HEADS)]
            o_rows.append(jnp.concatenate(
                [inter[hd] + _bdot(sc[hd], vg[:, sl]) for hd, sl in enumerate(heads)], axis=-1))
        pending = (si, t, jnp.concatenate(o_rows, axis=0))
    finish(*pending)


def _hgrn_mixer(x, g_pre, g_post, w_in_all, lb_logits, g_norm, w_out_all, *, layer, idx):
    bsz, seq, d = x.shape
    assert seq % HGRN_ROWS == 0 and HGRN_ROWS % HGRN_SUB == 0
    assert HGRN_SUB % (HGRN_GROUP * HGRN_CHUNK) == 0
    assert d == HGRN_HEADS * HEAD_DIM and w_in_all.shape[1:] == (d, 4 * d)
    sum_mat = _hgrn_sum_matrix()
    sum3 = jnp.asarray(np.concatenate([sum_mat] * 3, axis=1), jnp.bfloat16)
    g_norm_row = jnp.tile(g_norm.reshape(1, HEAD_DIM), (1, HGRN_HEADS))
    row = pl.BlockSpec((1, HGRN_ROWS, d), lambda b, s: (b, s, 0))
    vec = _resident((1, d))
    hbm = pl.BlockSpec(memory_space=pl.ANY)
    return pl.pallas_call(
        functools.partial(_hgrn_kernel, layer=layer, idx=idx),
        grid=(bsz, seq // HGRN_ROWS),
        in_specs=[row, vec, vec, hbm, _resident(lb_logits.shape), vec, hbm,
                  _resident(sum3.shape)],
        out_specs=row,
        out_shape=jax.ShapeDtypeStruct(x.shape, jnp.float32),
        scratch_shapes=[pltpu.VMEM((HGRN_HEADS, HEAD_DIM, HEAD_DIM), jnp.float32)]
        + _weight_scratch(w_in_all.shape[1:]) + _weight_scratch(w_out_all.shape[1:]),
        compiler_params=pltpu.CompilerParams(
            dimension_semantics=("arbitrary", "arbitrary"), vmem_limit_bytes=VMEM_LIMIT_BYTES),
        name="hgrn_mixer",
    )(x, g_pre, g_post, w_in_all, lb_logits, g_norm_row, w_out_all, sum3)


def kernel(x, norm_gains, ffn_w_in, ffn_w_out, conv_w_in, conv_b_in, conv_w_dw, conv_b_dw,
           conv_ln_g, conv_ln_b, conv_w_out, conv_b_out, hgrn_w_in, hgrn_lb_logits,
           hgrn_g_norm, hgrn_w_out):
    bsz, seq, d = x.shape
    depth = norm_gains.shape[0]
    vec = lambda v: v.reshape(1, -1).astype(jnp.float32)

    def ffn(x, layer, which):
        g = norm_gains[layer]
        y = _ffn(x.reshape(bsz * seq, d), vec(g[4 * which]), vec(g[4 * which + 1]),
                 ffn_w_in, ffn_w_out, layer=layer, which=which)
        return y.reshape(bsz, seq, d)

    for layer in range(depth):
        g = norm_gains[layer]
        x = ffn(x, layer, 0)
        j = layer // 2
        if layer % 2 == 0:
            x = _conv_mixer(x, vec(g[2]), vec(g[3]), conv_w_in, vec(conv_b_in[j]),
                            conv_w_dw[j], vec(conv_b_dw[j]), vec(conv_ln_g[j]),
                            vec(conv_ln_b[j]), conv_w_out, vec(conv_b_out[j]), idx=j)
        else:
            x = _hgrn_mixer(x, vec(g[2]), vec(g[3]), hgrn_w_in, hgrn_lb_logits,
                            hgrn_g_norm[j], hgrn_w_out, layer=layer, idx=j)
        x = ffn(x, layer, 1)
    return x
```
